```python
import jax, jax.numpy as jnp
from jax import lax
import numpy as np

D_MODEL = 1024
BATCH = 16
SEQ = 2048
DEPTH = 1
DEC_BATCH = 8
DEC_SEQ = 4096
PAST_LEN = 128

HEAD_DIM = 64
N_HEADS_A = D_MODEL // (2 * HEAD_DIM)
N_HEADS_B = D_MODEL // (2 * HEAD_DIM)
N_KV_B = 2
GQA_GROUP = N_HEADS_B // N_KV_B
WIDTH_A = N_HEADS_A * HEAD_DIM
WIDTH_B = N_HEADS_B * HEAD_DIM
MIX_WIDTH = WIDTH_A + WIDTH_B
KV_WIDTH_B = N_KV_B * HEAD_DIM
IN_COLS = 3 * WIDTH_A + WIDTH_B + 2 * KV_WIDTH_B
DILATED_BRANCHES = ((128, 1), (512, 4), (2048, 16))
WINDOW_B = 128
BLOCK_B = 128
ROPE_THETA = 500000.0
ROPE_DIM = HEAD_DIM // 4
D_FF = 4 * D_MODEL
EPS = 1e-6
NEG_BIG = -1e30

kernel_name = "hymba_dilated_swa_encoder"


def rmsnorm(x, g):
    xf = x.astype(jnp.float32)
    y = xf * lax.rsqrt(jnp.mean(xf * xf, axis=-1, keepdims=True) + EPS)
    return (y * g.astype(jnp.float32)).astype(x.dtype)


def partial_rope(x, pos):
    half = ROPE_DIM // 2
    inv = ROPE_THETA ** (-jnp.arange(half, dtype=jnp.float32) * 2.0 / ROPE_DIM)
    ang = pos[:, None] * inv[None, :]
    cos = jnp.cos(ang)[None, :, None, :]
    sin = jnp.sin(ang)[None, :, None, :]
    xf = x.astype(jnp.float32)
    x1, x2, xp = xf[..., :half], xf[..., half:ROPE_DIM], xf[..., ROPE_DIM:]
    out = jnp.concatenate([x1 * cos - x2 * sin, x2 * cos + x1 * sin, xp], axis=-1)
    return out.astype(x.dtype)


def banded_attention(q, k, v, half_window, block, sink=None):
    N, L, Hq, dh = q.shape
    Hk = k.shape[2]
    G = Hq // Hk
    nb = -(-L // block)
    Lp = nb * block
    pad = Lp - L
    qb = jnp.pad(q, ((0, 0), (0, pad), (0, 0), (0, 0))).reshape(N, nb, block, Hk, G, dh)
    kp = jnp.pad(k, ((0, 0), (block, pad + block), (0, 0), (0, 0))).reshape(N, nb + 2, block, Hk, dh)
    vp = jnp.pad(v, ((0, 0), (block, pad + block), (0, 0), (0, 0))).reshape(N, nb + 2, block, Hk, dh)
    kw = jnp.concatenate([kp[:, :-2], kp[:, 1:-1], kp[:, 2:]], axis=2)
    vw = jnp.concatenate([vp[:, :-2], vp[:, 1:-1], vp[:, 2:]], axis=2)
    a = jnp.arange(block)[None, :, None]
    c = jnp.arange(3 * block)[None, None, :]
    j = jnp.arange(nb)[:, None, None]
    qpos = j * block + a
    kpos = j * block + c - block
    mask = (jnp.abs(qpos - kpos) <= half_window) & (kpos >= 0) & (kpos < L)
    s = jnp.einsum('nbqhgd,nbkhd->nbhgqk', qb, kw).astype(jnp.float32) * (dh ** -0.5)
    s = jnp.where(mask[None, :, None, None, :, :], s, NEG_BIG)
    m = jnp.max(s, axis=-1, keepdims=True)
    if sink is not None:
        sk = sink.astype(jnp.float32).reshape(1, 1, Hk, G, 1, 1)
        m = jnp.maximum(m, sk)
    p = jnp.exp(s - m)
    l = jnp.sum(p, axis=-1, keepdims=True)
    if sink is not None:
        l = l + jnp.exp(sk - m)
    o = jnp.einsum('nbhgqk,nbkhd->nbqhgd', p, vw.astype(jnp.float32))
    l_t = jnp.moveaxis(l[..., 0], 4, 2)
    lse = jnp.moveaxis((m + jnp.log(l))[..., 0], 4, 2)
    o = (o / l_t[..., None]).reshape(N, Lp, Hq, dh)[:, :L].astype(q.dtype)
    lse = lse.reshape(N, Lp, Hq)[:, :L]
    return o, lse


def to_dilated(t, d):
    B, S = t.shape[:2]
    t = jnp.moveaxis(t.reshape(B, S // d, d, *t.shape[2:]), 2, 1)
    return t.reshape(B * d, S // d, *t.shape[3:])


def from_dilated(t, d, B):
    Nd, M = t.shape[:2]
    t = jnp.moveaxis(t.reshape(B, d, M, *t.shape[2:]), 1, 2)
    return t.reshape(B, M * d, *t.shape[3:])


def dilated_attention(q, k, v):
    B = q.shape[0]
    outs, lses = [], []
    for window, dil in DILATED_BRANCHES:
        hw = window // (2 * dil)
        o, lse = banded_attention(to_dilated(q, dil), to_dilated(k, dil), to_dilated(v, dil), hw, hw)
        outs.append(from_dilated(o, dil, B))
        lses.append(from_dilated(lse, dil, B))
    wts = jax.nn.softmax(jnp.stack(lses, axis=0), axis=0)
    o = jnp.sum(wts[..., None] * jnp.stack(outs, axis=0).astype(jnp.float32), axis=0)
    return o.astype(q.dtype)


def encoder_layer(x, norm_attn, w_in, q_norm_a, k_norm_a, q_norm_b, k_norm_b, sink_b,
                  out_norm_a, out_norm_b, w_o, norm_mlp, w_up, w_down):
    B, S, _ = x.shape
    pos = jnp.arange(S, dtype=jnp.float32)
    xn = rmsnorm(x, norm_attn)
    h = xn @ w_in
    o0 = 0
    def take(width, heads):
        nonlocal o0
        t = h[..., o0:o0 + width].reshape(B, S, heads, HEAD_DIM)
        o0 += width
        return t
    q_a = take(WIDTH_A, N_HEADS_A)
    k_a = take(WIDTH_A, N_HEADS_A)
    v_a = take(WIDTH_A, N_HEADS_A)
    q_b = take(WIDTH_B, N_HEADS_B)
    k_b = take(KV_WIDTH_B, N_KV_B)
    v_b = take(KV_WIDTH_B, N_KV_B)
    q_a = partial_rope(rmsnorm(q_a, q_norm_a), pos)
    k_a = partial_rope(rmsnorm(k_a, k_norm_a), pos)
    q_b = partial_rope(rmsnorm(q_b, q_norm_b), pos)
    k_b = partial_rope(rmsnorm(k_b, k_norm_b), pos)
    o_a = dilated_attention(q_a, k_a, v_a).reshape(B, S, WIDTH_A)
    o_b, _ = banded_attention(q_b, k_b, v_b, WINDOW_B, BLOCK_B, sink=sink_b)
    o_b = o_b.reshape(B, S, WIDTH_B)
    mix = jnp.concatenate([rmsnorm(o_a, out_norm_a), rmsnorm(o_b, out_norm_b)], axis=-1)
    x = x + mix @ w_o
    hn = rmsnorm(x, norm_mlp)
    u = jax.nn.relu(hn @ w_up)
    return x + (u * u) @ w_down


def setup_inputs(seed: int = 0) -> dict:
    key = jax.random.key(seed)
    ks = jax.random.split(key, 16)
    f32 = jnp.float32
    def gain(k, n):
        return 1.0 + 0.01 * jax.random.normal(k, (DEPTH, n), f32)
    return {
        "x_prompt": jax.random.normal(ks[0], (BATCH, SEQ, D_MODEL), f32),
        "x_sample": jax.random.normal(ks[1], (DEC_BATCH, DEC_SEQ, D_MODEL), f32),
        "norm_attn": gain(ks[2], D_MODEL),
        "w_in": jax.random.normal(ks[3], (DEPTH, D_MODEL, IN_COLS), f32) * D_MODEL ** -0.5,
        "q_norm_a": gain(ks[4], HEAD_DIM),
        "k_norm_a": gain(ks[5], HEAD_DIM),
        "q_norm_b": gain(ks[6], HEAD_DIM),
        "k_norm_b": gain(ks[7], HEAD_DIM),
        "sink_b": 0.5 * jax.random.normal(ks[8], (DEPTH, N_HEADS_B), f32),
        "out_norm_a": gain(ks[9], WIDTH_A),
        "out_norm_b": gain(ks[10], WIDTH_B),
        "w_o": jax.random.normal(ks[11], (DEPTH, MIX_WIDTH, D_MODEL), f32) * MIX_WIDTH ** -0.5,
        "norm_mlp": gain(ks[12], D_MODEL),
        "w_up": jax.random.normal(ks[13], (DEPTH, D_MODEL, D_FF), f32) * D_MODEL ** -0.5,
        "w_down": jax.random.normal(ks[14], (DEPTH, D_FF, D_MODEL), f32) * D_FF ** -0.5,
    }


def reference(x_prompt, x_sample, norm_attn, w_in, q_norm_a, k_norm_a, q_norm_b, k_norm_b,
              sink_b, out_norm_a, out_norm_b, w_o, norm_mlp, w_up, w_down):
    yp = x_prompt
    ys = x_sample
    for l in range(DEPTH):
        params = (norm_attn[l], w_in[l], q_norm_a[l], k_norm_a[l], q_norm_b[l], k_norm_b[l],
                  sink_b[l], out_norm_a[l], out_norm_b[l], w_o[l], norm_mlp[l], w_up[l], w_down[l])
        yp = encoder_layer(yp, *params)
        ys = encoder_layer(ys, *params)
    return (yp, ys)
```

```python
import functools

import jax
import jax.numpy as jnp
from jax import lax
from jax.experimental import pallas as pl
from jax.experimental.pallas import tpu as pltpu

HEAD_DIM = 64
PAIR = 2 * HEAD_DIM
DILATED_BRANCHES = ((128, 1), (512, 4), (2048, 16))
WINDOW_B = 128
ROPE_THETA = 500000.0
ROPE_DIM = HEAD_DIM // 4
ROPE_HALF = ROPE_DIM // 2
EPS = 1e-6
NEG_BIG = -1e30
LANES = 128
V7X_VMEM_BYTES = 64 * 1024 * 1024
VMEM_LIMIT = V7X_VMEM_BYTES - 8 * 1024 * 1024

F32 = jnp.float32
BF16 = jnp.bfloat16


def _dot(a, b):
    return jnp.dot(a, b, preferred_element_type=F32)


def _dot_nt(a, b):
    return lax.dot_general(a, b, (((1,), (1,)), ((), ())), preferred_element_type=F32)


def _proj_kernel(x_ref, g_ref, w_ref, gcat_ref, bd_ref, c_ref, s1_ref, s2_ref,
                 qa1, ka1, va1, qa4, ka4, va4, qa16, ka16, va16, qb, kb, vb,
                 stage_ref, *, width_a, width_b, kv_b):
    tm = x_ref.shape[0]
    x = x_ref[...]
    ms = jnp.mean(x * x, axis=-1, keepdims=True)
    xn = (x * lax.rsqrt(ms + EPS) * g_ref[...]).astype(BF16)
    cos = c_ref[...]
    sin_up = s1_ref[...]
    sin_dn = s2_ref[...]
    bd = bd_ref[...]

    def head_norm_rope(col0, width):
        h = _dot(xn, w_ref[:, col0:col0 + width])
        sq = (h * h).astype(BF16)
        blocks = []
        sub = min(width, bd.shape[0])
        for c in range(0, width, sub):
            msq = _dot(sq[:, c:c + sub], bd[:sub, :sub]) * (1.0 / HEAD_DIM)
            y = h[:, c:c + sub] * lax.rsqrt(msq + EPS) * gcat_ref[:, col0 + c:col0 + c + sub]
            for b in range(0, sub, LANES):
                yb = y[:, b:b + LANES]
                up = pltpu.roll(yb, LANES - ROPE_HALF, 1)
                dn = pltpu.roll(yb, ROPE_HALF, 1)
                blocks.append(yb * cos + up * sin_up + dn * sin_dn)
        return blocks

    def write_dilated(blocks, outs):
        o1, o4, o16 = outs
        for p, blk in enumerate(blocks):
            o1[p, 0] = blk.astype(BF16)
            stage_ref[...] = blk
            for o, d in ((o4, 4), (o16, 16)):
                for r in range(d):
                    o[p, r] = stage_ref[pl.ds(r, tm // d, stride=d), :].astype(BF16)

    c0 = 0
    write_dilated(head_norm_rope(c0, width_a), (qa1, qa4, qa16))
    c0 += width_a
    write_dilated(head_norm_rope(c0, width_a), (ka1, ka4, ka16))
    c0 += width_a
    for p, blk in enumerate(head_norm_rope(c0, width_b)):
        qb[p] = blk.astype(BF16)
    c0 += width_b
    (kblk,) = head_norm_rope(c0, kv_b)
    c0 += kv_b
    hv = _dot(xn, w_ref[:, c0:c0 + width_a])
    write_dilated([hv[:, b:b + LANES] for b in range(0, width_a, LANES)], (va1, va4, va16))
    c0 += width_a
    vblk = _dot(xn, w_ref[:, c0:c0 + kv_b])

    lo = lax.broadcasted_iota(jnp.int32, (tm, LANES), 1) < HEAD_DIM
    for blk, out in ((kblk, kb), (vblk, vb)):
        sw = pltpu.roll(blk, HEAD_DIM, 1)
        out[0] = jnp.where(lo, blk, sw).astype(BF16)
        out[1] = jnp.where(lo, sw, blk).astype(BF16)


def _proj_call(x, g, w, gcat, bd, cos, sin_up, sin_dn, *, width_a, width_b, kv_b, tm):
    B, S, D = x.shape
    n_pa = width_a // PAIR
    n_pb = width_b // PAIR
    n_kv = kv_b // HEAD_DIM
    cols = w.shape[1]

    def dil_shape(d):
        return jax.ShapeDtypeStruct((B, n_pa, d, S // d, PAIR), BF16)

    def dil_spec(d):
        return pl.BlockSpec((None, n_pa, d, tm // d, PAIR), lambda b, i: (b, 0, 0, i, 0))

    const = lambda b, i: (0, 0)
    out_shape = []
    out_specs = []
    for d in (1, 4, 16):
        out_shape += [dil_shape(d)] * 3
        out_specs += [dil_spec(d)] * 3
    out_shape += [jax.ShapeDtypeStruct((B, n_pb, S, PAIR), BF16),
                  jax.ShapeDtypeStruct((B, n_kv, S, PAIR), BF16),
                  jax.ShapeDtypeStruct((B, n_kv, S, PAIR), BF16)]
    out_specs += [pl.BlockSpec((None, n_pb, tm, PAIR), lambda b, i: (b, 0, i, 0)),
                  pl.BlockSpec((None, n_kv, tm, PAIR), lambda b, i: (b, 0, i, 0)),
                  pl.BlockSpec((None, n_kv, tm, PAIR), lambda b, i: (b, 0, i, 0))]
    tab_spec = pl.BlockSpec((tm, LANES), lambda b, i: (i, 0))
    return pl.pallas_call(
        functools.partial(_proj_kernel, width_a=width_a, width_b=width_b, kv_b=kv_b),
        grid=(B, S // tm),
        in_specs=[pl.BlockSpec((None, tm, D), lambda b, i: (b, i, 0)),
                  pl.BlockSpec((1, D), const),
                  pl.BlockSpec((D, cols), const, pipeline_mode=pl.Buffered(1)),
                  pl.BlockSpec((1, gcat.shape[1]), const),
                  pl.BlockSpec(bd.shape, const),
                  tab_spec, tab_spec, tab_spec],
        out_specs=out_specs,
        out_shape=out_shape,
        scratch_shapes=[pltpu.VMEM((tm, LANES), F32)],
        compiler_params=pltpu.CompilerParams(
            dimension_semantics=("parallel", "parallel"), vmem_limit_bytes=VMEM_LIMIT),
        name="proj_qkv",
    )(x, g, w, gcat, bd, cos, sin_up, sin_dn)


def _attend(q_tiles, k, v, mask, sinks):
    bq = q_tiles[0].shape[0]
    lo = lax.broadcasted_iota(jnp.int32, (bq, PAIR), 1) < HEAD_DIM
    zero = jnp.zeros((bq, PAIR), BF16)
    rows = []
    for q in q_tiles:
        rows.append(jnp.where(lo, q, zero))
        rows.append(jnp.where(lo, zero, q))
    s = _dot_nt(jnp.concatenate(rows, axis=0), k)
    ps, ms, ls = [], [], []
    for h in range(len(rows)):
        sh = jnp.where(mask, s[h * bq:(h + 1) * bq], NEG_BIG)
        m = jnp.max(sh, axis=-1, keepdims=True)
        if sinks is not None:
            m = jnp.maximum(m, sinks[h])
        p = jnp.exp(sh - m)
        l = jnp.sum(p, axis=-1, keepdims=True)
        if sinks is not None:
            l = l + jnp.exp(sinks[h] - m)
        ps.append(p.astype(BF16))
        ms.append(m)
        ls.append(l)
    pv = _dot(jnp.concatenate(ps, axis=0), v)
    out = []
    for t in range(len(q_tiles)):
        a, b = 2 * t, 2 * t + 1
        acc = jnp.where(lo, pv[a * bq:(a + 1) * bq], pv[b * bq:(b + 1) * bq])
        m = jnp.where(lo, ms[a], ms[b])
        l = jnp.where(lo, ls[a], ls[b])
        out.append((acc, m, l))
    return out


def _band_mask(bq, tk, offset, half_window):
    r = lax.broadcasted_iota(jnp.int32, (bq, tk), 0)
    c = lax.broadcasted_iota(jnp.int32, (bq, tk), 1)
    return jnp.abs(c - r + offset) <= half_window


def _attn_a_kernel(q1, k1, v1, q4, k4, v4, q16, k16, v16, o_ref, acc_ref, m_ref, l_ref):
    S = o_ref.shape[0]
    branches = ((q1, k1, v1), (q4, k4, v4), (q16, k16, v16))
    for bi, ((window, d), (q_ref, k_ref, v_ref)) in enumerate(zip(DILATED_BRANCHES, branches)):
        hw = window // (2 * d)
        seg = S // d
        bq = min(2 * hw, seg)
        tk = min(bq + 2 * hw, seg)
        first = bi == 0
        last = bi == len(branches) - 1

        def step(j, carry, d=d, hw=hw, seg=seg, bq=bq, tk=tk, first=first, last=last,
                 q_ref=q_ref, k_ref=k_ref, v_ref=v_ref):
            i0 = pl.multiple_of(j * bq, bq)
            r = i0 // seg
            m0 = i0 - r * seg
            ks = pl.multiple_of(r * seg + jnp.clip(m0 - hw, 0, seg - tk), hw)
            mask = _band_mask(bq, tk, ks - i0, hw)
            ((acc, m, l),) = _attend([q_ref[pl.ds(i0, bq), :]], k_ref[pl.ds(ks, tk), :],
                                     v_ref[pl.ds(ks, tk), :], mask, None)
            if d == 1:
                rows = pl.ds(i0, bq)
            else:
                rows = pl.ds(r + d * m0, bq, stride=d)
            if not first:
                m_old = m_ref[rows, :]
                m_new = jnp.maximum(m_old, m)
                a_old = jnp.exp(m_old - m_new)
                a_cur = jnp.exp(m - m_new)
                acc = a_old * acc_ref[rows, :] + a_cur * acc
                l = a_old * l_ref[rows, :] + a_cur * l
                m = m_new
            if last:
                o_ref[rows, :] = acc / l
            else:
                acc_ref[rows, :] = acc
                m_ref[rows, :] = m
                l_ref[rows, :] = l
            return carry

        lax.fori_loop(0, S // bq, step, 0)


def _attn_a_call(qkv, B, S, width_a):
    n_p = width_a // PAIR
    flat = [t.reshape(B, n_p, S, PAIR) for t in qkv]
    spec = pl.BlockSpec((None, None, S, PAIR), lambda b, p: (b, p, 0, 0))
    return pl.pallas_call(
        _attn_a_kernel,
        grid=(B, n_p),
        in_specs=[spec] * 9,
        out_specs=pl.BlockSpec((None, S, PAIR), lambda b, p: (b, 0, p)),
        out_shape=jax.ShapeDtypeStruct((B, S, width_a), F32),
        scratch_shapes=[pltpu.VMEM((S, PAIR), F32)] * 3,
        compiler_params=pltpu.CompilerParams(
            dimension_semantics=("parallel", "parallel"), vmem_limit_bytes=VMEM_LIMIT),
        name="attn_dilated",
    )(*flat)


def _attn_b_kernel(sink_ref, q_ref, k_ref, v_ref, o_ref):
    n_t, S, _ = q_ref.shape
    g = pl.program_id(1)
    bq = WINDOW_B
    tk = min(3 * bq, S)
    sinks = [sink_ref[g * 2 * n_t + h] for h in range(2 * n_t)]

    def step(j, carry):
        i0 = pl.multiple_of(j * bq, bq)
        ks = pl.multiple_of(jnp.clip(i0 - WINDOW_B, 0, S - tk), WINDOW_B)
        mask = _band_mask(bq, tk, ks - i0, WINDOW_B)
        res = _attend([q_ref[t, pl.ds(i0, bq), :] for t in range(n_t)],
                      k_ref[pl.ds(ks, tk), :], v_ref[pl.ds(ks, tk), :], mask, sinks)
        for t, (acc, m, l) in enumerate(res):
            o_ref[pl.ds(i0, bq), t * PAIR:(t + 1) * PAIR] = acc / l
        return carry

    lax.fori_loop(0, S // bq, step, 0)


def _attn_b_call(sink, qb, kb, vb):
    B, n_p, S, _ = qb.shape
    n_kv = kb.shape[1]
    n_t = n_p // n_kv
    kv_spec = pl.BlockSpec((None, None, S, PAIR), lambda b, g: (b, g, 0, 0))
    return pl.pallas_call(
        _attn_b_kernel,
        grid=(B, n_kv),
        in_specs=[pl.BlockSpec(memory_space=pltpu.SMEM),
                  pl.BlockSpec((None, n_t, S, PAIR), lambda b, g: (b, g, 0, 0)),
                  kv_spec, kv_spec],
        out_specs=pl.BlockSpec((None, S, n_t * PAIR), lambda b, g: (b, 0, g)),
        out_shape=jax.ShapeDtypeStruct((B, S, n_p * PAIR), F32),
        compiler_params=pltpu.CompilerParams(
            dimension_semantics=("parallel", "parallel"), vmem_limit_bytes=VMEM_LIMIT),
        name="attn_window",
    )(sink, qb, kb, vb)


def _rms(x, g):
    ms = jnp.mean(x * x, axis=-1, keepdims=True)
    return x * lax.rsqrt(ms + EPS) * g


def _mlp_kernel(oa_ref, ob_ref, x_ref, ga_ref, gb_ref, wo_ref, gm_ref, wu_ref, wd_ref, y_ref,
                *, ff_chunk):
    wa = oa_ref.shape[1]
    na = _rms(oa_ref[...], ga_ref[...]).astype(BF16)
    nb = _rms(ob_ref[...], gb_ref[...]).astype(BF16)
    h = x_ref[...] + _dot(na, wo_ref[:wa, :]) + _dot(nb, wo_ref[wa:, :])
    hn = _rms(h, gm_ref[...]).astype(BF16)
    acc = None
    for c in range(0, wu_ref.shape[1], ff_chunk):
        u = jnp.maximum(_dot(hn, wu_ref[:, c:c + ff_chunk]), 0.0)
        d = _dot((u * u).astype(BF16), wd_ref[c:c + ff_chunk, :])
        acc = d if acc is None else acc + d
    y_ref[...] = h + acc


def _mlp_call(oa, ob, x, ga, gb, wo, gm, wu, wd, *, tm, ff_chunk):
    B, S, D = x.shape
    const = lambda b, i: (0, 0)

    def resident(a):
        return pl.BlockSpec(a.shape, const, pipeline_mode=pl.Buffered(1))

    def rows(a):
        return pl.BlockSpec((None, tm, a.shape[2]), lambda b, i: (b, i, 0))

    return pl.pallas_call(
        functools.partial(_mlp_kernel, ff_chunk=ff_chunk),
        grid=(B, S // tm),
        in_specs=[rows(oa), rows(ob), rows(x), resident(ga), resident(gb), resident(wo),
                  resident(gm), resident(wu), resident(wd)],
        out_specs=rows(x),
        out_shape=jax.ShapeDtypeStruct((B, S, D), F32),
        compiler_params=pltpu.CompilerParams(
            dimension_semantics=("parallel", "parallel"), vmem_limit_bytes=VMEM_LIMIT),
        name="out_mlp",
    )(oa, ob, x, ga, gb, wo, gm, wu, wd)


def _rope_tables(S):
    inv = ROPE_THETA ** (-jnp.arange(ROPE_HALF, dtype=F32) * 2.0 / ROPE_DIM)
    ang = jnp.arange(S, dtype=F32)[:, None] * inv[None, :]
    cos, sin = jnp.cos(ang), jnp.sin(ang)
    ones = jnp.ones((S, HEAD_DIM - ROPE_DIM), F32)
    zeros = jnp.zeros((S, HEAD_DIM - ROPE_DIM), F32)
    z8 = jnp.zeros((S, ROPE_HALF), F32)
    head_cos = jnp.concatenate([cos, cos, ones], axis=1)
    head_up = jnp.concatenate([-sin, z8, zeros], axis=1)
    head_dn = jnp.concatenate([z8, sin, zeros], axis=1)
    rep = LANES // HEAD_DIM
    return tuple(jnp.tile(t, (1, rep)) for t in (head_cos, head_up, head_dn))


def _layer(x, p, *, tm_proj, tm_mlp, ff_chunk):
    B, S, D = x.shape
    width_a, width_b, kv_b = p["width_a"], p["width_b"], p["kv_b"]
    cos, sin_up, sin_dn = _rope_tables(S)
    outs = _proj_call(x, p["norm_attn"], p["w_in"], p["gcat"], p["bd"], cos, sin_up, sin_dn,
                      width_a=width_a, width_b=width_b, kv_b=kv_b, tm=tm_proj)
    qa1, ka1, va1, qa4, ka4, va4, qa16, ka16, va16, qb, kb, vb = outs
    oa = _attn_a_call((qa1, ka1, va1, qa4, ka4, va4, qa16, ka16, va16), B, S, width_a)
    ob = _attn_b_call(p["sink"], qb, kb, vb)
    return _mlp_call(oa, ob, x, p["out_norm_a"], p["out_norm_b"], p["w_o"], p["norm_mlp"],
                     p["w_up"], p["w_down"], tm=tm_mlp, ff_chunk=ff_chunk)


def _prepare(norm_attn, w_in, q_norm_a, k_norm_a, q_norm_b, k_norm_b, sink_b, out_norm_a,
             out_norm_b, w_o, norm_mlp, w_up, w_down):
    width_a = out_norm_a.shape[0]
    width_b = out_norm_b.shape[0]
    kv_b = (w_in.shape[1] - 3 * width_a - width_b) // 2
    o_qa, o_ka, o_va = 0, width_a, 2 * width_a
    o_qb = 3 * width_a
    o_kb = o_qb + width_b
    o_vb = o_kb + kv_b
    w = jnp.concatenate([w_in[:, o_qa:o_qa + width_a], w_in[:, o_ka:o_ka + width_a],
                         w_in[:, o_qb:o_qb + width_b], w_in[:, o_kb:o_kb + kv_b],
                         w_in[:, o_va:o_va + width_a], w_in[:, o_vb:o_vb + kv_b]], axis=1)
    scale = HEAD_DIM ** -0.5
    gcat = jnp.concatenate([jnp.tile(q_norm_a, width_a // HEAD_DIM) * scale,
                            jnp.tile(k_norm_a, width_a // HEAD_DIM),
                            jnp.tile(q_norm_b, width_b // HEAD_DIM) * scale,
                            jnp.tile(k_norm_b, kv_b // HEAD_DIM)])[None, :]
    n = 2 * LANES
    bd = (jnp.arange(n)[:, None] // HEAD_DIM == jnp.arange(n)[None, :] // HEAD_DIM).astype(BF16)
    return dict(width_a=width_a, width_b=width_b, kv_b=kv_b,
                norm_attn=norm_attn[None, :], w_in=w.astype(BF16), gcat=gcat, bd=bd,
                sink=sink_b, out_norm_a=out_norm_a[None, :], out_norm_b=out_norm_b[None, :],
                w_o=w_o.astype(BF16), norm_mlp=norm_mlp[None, :],
                w_up=w_up.astype(BF16), w_down=w_down.astype(BF16))


def kernel(x_prompt, x_sample, norm_attn, w_in, q_norm_a, k_norm_a, q_norm_b, k_norm_b, sink_b,
           out_norm_a, out_norm_b, w_o, norm_mlp, w_up, w_down):
    yp, ys = x_prompt, x_sample
    for l in range(w_in.shape[0]):
        p = _prepare(norm_attn[l], w_in[l], q_norm_a[l], k_norm_a[l], q_norm_b[l], k_norm_b[l],
                     sink_b[l], out_norm_a[l], out_norm_b[l], w_o[l], norm_mlp[l], w_up[l],
                     w_down[l])
        yp = _layer(yp, p, tm_proj=512, tm_mlp=512, ff_chunk=512)
        ys = _layer(ys, p, tm_proj=512, tm_mlp=512, ff_chunk=512)
    return (yp, ys)
```

```python
import functools

import jax
import jax.numpy as jnp
from jax import lax
from jax.experimental import pallas as pl
from jax.experimental.pallas import tpu as pltpu

HEAD_DIM = 64
PAIR = 2 * HEAD_DIM
DILATED_BRANCHES = ((128, 1), (512, 4), (2048, 16))
WINDOW_B = 128
ROPE_THETA = 500000.0
ROPE_DIM = HEAD_DIM // 4
ROPE_HALF = ROPE_DIM // 2
EPS = 1e-6
NEG_BIG = -1e30
LANES = 128
V7X_VMEM_BYTES = 64 * 1024 * 1024
VMEM_LIMIT = V7X_VMEM_BYTES - 8 * 1024 * 1024

F32 = jnp.float32
BF16 = jnp.bfloat16


def _dot(a, b):
    return jnp.dot(a, b, preferred_element_type=F32)


def _dot_nt(a, b):
    return lax.dot_general(a, b, (((1,), (1,)), ((), ())), preferred_element_type=F32)


def _proj_kernel(x_ref, g_ref, w_ref, gcat_ref, bd_ref, c_ref, s1_ref, s2_ref,
                 qa1, ka1, va1, qa4, ka4, va4, qa16, ka16, va16, qb, kb, vb,
                 stage_ref, *, width_a, width_b, kv_b):
    tm = x_ref.shape[0]
    x = x_ref[...]
    ms = jnp.mean(x * x, axis=-1, keepdims=True)
    xn = (x * lax.rsqrt(ms + EPS) * g_ref[...]).astype(BF16)
    cos = c_ref[...]
    sin_up = s1_ref[...]
    sin_dn = s2_ref[...]
    bd = bd_ref[...]

    def head_norm_rope(col0, width):
        h = _dot(xn, w_ref[:, col0:col0 + width])
        sq = (h * h).astype(BF16)
        blocks = []
        sub = min(width, bd.shape[0])
        for c in range(0, width, sub):
            msq = _dot(sq[:, c:c + sub], bd[:sub, :sub]) * (1.0 / HEAD_DIM)
            y = h[:, c:c + sub] * lax.rsqrt(msq + EPS) * gcat_ref[:, col0 + c:col0 + c + sub]
            for b in range(0, sub, LANES):
                yb = y[:, b:b + LANES]
                up = pltpu.roll(yb, LANES - ROPE_HALF, 1)
                dn = pltpu.roll(yb, ROPE_HALF, 1)
                blocks.append(yb * cos + up * sin_up + dn * sin_dn)
        return blocks

    def write_dilated(blocks, outs):
        o1, o4, o16 = outs
        for p, blk in enumerate(blocks):
            o1[p, 0] = blk.astype(BF16)
            stage_ref[...] = blk
            for o, d in ((o4, 4), (o16, 16)):
                for r in range(d):
                    o[p, r] = stage_ref[pl.ds(r, tm // d, stride=d), :].astype(BF16)

    c0 = 0
    write_dilated(head_norm_rope(c0, width_a), (qa1, qa4, qa16))
    c0 += width_a
    write_dilated(head_norm_rope(c0, width_a), (ka1, ka4, ka16))
    c0 += width_a
    for p, blk in enumerate(head_norm_rope(c0, width_b)):
        qb[p] = blk.astype(BF16)
    c0 += width_b
    (kblk,) = head_norm_rope(c0, kv_b)
    c0 += kv_b
    hv = _dot(xn, w_ref[:, c0:c0 + width_a])
    write_dilated([hv[:, b:b + LANES] for b in range(0, width_a, LANES)], (va1, va4, va16))
    c0 += width_a
    vblk = _dot(xn, w_ref[:, c0:c0 + kv_b])

    lo = lax.broadcasted_iota(jnp.int32, (tm, LANES), 1) < HEAD_DIM
    for blk, out in ((kblk, kb), (vblk, vb)):
        sw = pltpu.roll(blk, HEAD_DIM, 1)
        out[0] = jnp.where(lo, blk, sw).astype(BF16)
        out[1] = jnp.where(lo, sw, blk).astype(BF16)


def _proj_call(x, g, w, gcat, bd, cos, sin_up, sin_dn, *, width_a, width_b, kv_b, tm):
    B, S, D = x.shape
    n_pa = width_a // PAIR
    n_pb = width_b // PAIR
    n_kv = kv_b // HEAD_DIM
    cols = w.shape[1]

    def dil_shape(d):
        return jax.ShapeDtypeStruct((B, n_pa, d, S // d, PAIR), BF16)

    def dil_spec(d):
        return pl.BlockSpec((None, n_pa, d, tm // d, PAIR), lambda b, i: (b, 0, 0, i, 0))

    const = lambda b, i: (0, 0)
    out_shape = []
    out_specs = []
    for d in (1, 4, 16):
        out_shape += [dil_shape(d)] * 3
        out_specs += [dil_spec(d)] * 3
    out_shape += [jax.ShapeDtypeStruct((B, n_pb, S, PAIR), BF16),
                  jax.ShapeDtypeStruct((B, n_kv, S, PAIR), BF16),
                  jax.ShapeDtypeStruct((B, n_kv, S, PAIR), BF16)]
    out_specs += [pl.BlockSpec((None, n_pb, tm, PAIR), lambda b, i: (b, 0, i, 0)),
                  pl.BlockSpec((None, n_kv, tm, PAIR), lambda b, i: (b, 0, i, 0)),
                  pl.BlockSpec((None, n_kv, tm, PAIR), lambda b, i: (b, 0, i, 0))]
    tab_spec = pl.BlockSpec((tm, LANES), lambda b, i: (i, 0))
    return pl.pallas_call(
        functools.partial(_proj_kernel, width_a=width_a, width_b=width_b, kv_b=kv_b),
        grid=(B, S // tm),
        in_specs=[pl.BlockSpec((None, tm, D), lambda b, i: (b, i, 0)),
                  pl.BlockSpec((1, D), const),
                  pl.BlockSpec((D, cols), const, pipeline_mode=pl.Buffered(1)),
                  pl.BlockSpec((1, gcat.shape[1]), const),
                  pl.BlockSpec(bd.shape, const),
                  tab_spec, tab_spec, tab_spec],
        out_specs=out_specs,
        out_shape=out_shape,
        scratch_shapes=[pltpu.VMEM((tm, LANES), F32)],
        compiler_params=pltpu.CompilerParams(
            dimension_semantics=("parallel", "parallel"), vmem_limit_bytes=VMEM_LIMIT),
        name="proj_qkv",
    )(x, g, w, gcat, bd, cos, sin_up, sin_dn)


def _attend(blocks, sinks):
    bq = blocks[0][0][0].shape[0]
    lo = lax.broadcasted_iota(jnp.int32, (bq, PAIR), 1) < HEAD_DIM
    zero = jnp.zeros((bq, PAIR), BF16)
    scores = []
    for q_tiles, k, _, _ in blocks:
        rows = []
        for q in q_tiles:
            rows.append(jnp.where(lo, q, zero))
            rows.append(jnp.where(lo, zero, q))
        scores.append(_dot_nt(jnp.concatenate(rows, axis=0), k))
    probs = []
    for (q_tiles, _, _, bias), s in zip(blocks, scores):
        ps, ms = [], []
        for h in range(2 * len(q_tiles)):
            sh = s[h * bq:(h + 1) * bq] + bias
            m = jnp.max(sh, axis=-1, keepdims=True)
            if sinks is not None:
                m = jnp.maximum(m, sinks[h])
            ps.append(jnp.exp(sh - m).astype(BF16))
            ms.append(m)
        probs.append((jnp.concatenate(ps, axis=0), ms))
    out = []
    for (q_tiles, _, v, _), (p, ms) in zip(blocks, probs):
        pv = _dot(p, jnp.concatenate([v, jnp.ones_like(v)], axis=1))
        res = []
        for t in range(len(q_tiles)):
            a, b = 2 * t, 2 * t + 1
            pa, pb = pv[a * bq:(a + 1) * bq], pv[b * bq:(b + 1) * bq]
            la, lb = pa[:, PAIR:], pb[:, PAIR:]
            if sinks is not None:
                la = la + jnp.exp(sinks[a] - ms[a])
                lb = lb + jnp.exp(sinks[b] - ms[b])
            res.append((jnp.where(lo, pa[:, :PAIR], pb[:, :PAIR]),
                        jnp.where(lo, ms[a], ms[b]), jnp.where(lo, la, lb)))
        out.append(res)
    return out


def _band_mask(bq, tk, offset, half_window):
    r = lax.broadcasted_iota(jnp.int32, (bq, tk), 0)
    c = lax.broadcasted_iota(jnp.int32, (bq, tk), 1)
    return jnp.abs(c - r + offset) <= half_window


def _branch_cfg(S, window, d):
    hw = window // (2 * d)
    seg = S // d
    bq = min(2 * hw, seg)
    tk = min(bq + 2 * hw, seg)
    return hw, seg, bq, tk


def _fill_bias(bias_ref, bq, tk, hw):
    for n in range(bias_ref.shape[0]):
        bias_ref[n] = jnp.where(_band_mask(bq, tk, -n * hw, hw), 0.0, NEG_BIG)


def _attn_a_kernel(q1, k1, v1, q4, k4, v4, q16, k16, v16, o_ref, acc_ref, m_ref, l_ref,
                   *bias_refs, group):
    S = o_ref.shape[0]
    branches = ((q1, k1, v1), (q4, k4, v4), (q16, k16, v16))
    for bi, ((window, d), (q_ref, k_ref, v_ref)) in enumerate(zip(DILATED_BRANCHES, branches)):
        hw, seg, bq, tk = _branch_cfg(S, window, d)
        _fill_bias(bias_refs[bi], bq, tk, hw)
        first = bi == 0
        last = bi == len(branches) - 1

        def step(j, carry, d=d, hw=hw, seg=seg, bq=bq, tk=tk, first=first, last=last,
                 q_ref=q_ref, k_ref=k_ref, v_ref=v_ref, bias_ref=bias_refs[bi]):
            blocks, dests = [], []
            for g in range(group):
                i0 = pl.multiple_of((j * group + g) * bq, bq)
                r = i0 // seg
                m0 = i0 - r * seg
                back = m0 - jnp.clip(m0 - hw, 0, seg - tk)
                ks = pl.multiple_of(i0 - back, hw)
                blocks.append(([q_ref[pl.ds(i0, bq), :]], k_ref[pl.ds(ks, tk), :],
                               v_ref[pl.ds(ks, tk), :], bias_ref[back // hw]))
                dests.append(pl.ds(i0, bq) if d == 1 else pl.ds(r + d * m0, bq, stride=d))
            for rows, ((acc, m, l),) in zip(dests, _attend(blocks, None)):
                if not first:
                    m_old = m_ref[rows, :]
                    m_new = jnp.maximum(m_old, m)
                    a_old = jnp.exp(m_old - m_new)
                    a_cur = jnp.exp(m - m_new)
                    acc = a_old * acc_ref[rows, :] + a_cur * acc
                    l = a_old * l_ref[rows, :] + a_cur * l
                    m = m_new
                if last:
                    o_ref[rows, :] = acc / l
                else:
                    acc_ref[rows, :] = acc
                    m_ref[rows, :] = m
                    l_ref[rows, :] = l
            return carry

        lax.fori_loop(0, S // (bq * group), step, 0)


def _attn_a_call(qkv, B, S, width_a, *, group):
    n_p = width_a // PAIR
    flat = [t.reshape(B, n_p, S, PAIR) for t in qkv]
    spec = pl.BlockSpec((None, None, S, PAIR), lambda b, p: (b, p, 0, 0))
    bias_shapes = []
    for window, d in DILATED_BRANCHES:
        hw, seg, bq, tk = _branch_cfg(S, window, d)
        bias_shapes.append(pltpu.VMEM(((tk - bq) // hw + 1, bq, tk), F32))
    return pl.pallas_call(
        functools.partial(_attn_a_kernel, group=group),
        grid=(B, n_p),
        in_specs=[spec] * 9,
        out_specs=pl.BlockSpec((None, S, PAIR), lambda b, p: (b, 0, p)),
        out_shape=jax.ShapeDtypeStruct((B, S, width_a), F32),
        scratch_shapes=[pltpu.VMEM((S, PAIR), F32)] * 3 + bias_shapes,
        compiler_params=pltpu.CompilerParams(
            dimension_semantics=("parallel", "parallel"), vmem_limit_bytes=VMEM_LIMIT),
        name="attn_dilated",
    )(*flat)


def _attn_b_kernel(sink_ref, q_ref, k_ref, v_ref, o_ref, bias_ref, *, group):
    n_t, S, _ = q_ref.shape
    kv_head = pl.program_id(1)
    bq = WINDOW_B
    tk = min(3 * bq, S)
    sinks = [sink_ref[kv_head * 2 * n_t + h] for h in range(2 * n_t)]
    _fill_bias(bias_ref, bq, tk, WINDOW_B)

    def step(j, carry):
        blocks, starts = [], []
        for g in range(group):
            i0 = pl.multiple_of((j * group + g) * bq, bq)
            ks = pl.multiple_of(jnp.clip(i0 - WINDOW_B, 0, S - tk), WINDOW_B)
            blocks.append(([q_ref[t, pl.ds(i0, bq), :] for t in range(n_t)],
                           k_ref[pl.ds(ks, tk), :], v_ref[pl.ds(ks, tk), :],
                           bias_ref[(i0 - ks) // WINDOW_B]))
            starts.append(i0)
        for i0, res in zip(starts, _attend(blocks, sinks)):
            for t, (acc, m, l) in enumerate(res):
                o_ref[pl.ds(i0, bq), t * PAIR:(t + 1) * PAIR] = acc / l
        return carry

    lax.fori_loop(0, S // (bq * group), step, 0)


def _attn_b_call(sink, qb, kb, vb, *, group):
    B, n_p, S, _ = qb.shape
    n_kv = kb.shape[1]
    n_t = n_p // n_kv
    kv_spec = pl.BlockSpec((None, None, S, PAIR), lambda b, g: (b, g, 0, 0))
    bq = WINDOW_B
    tk = min(3 * bq, S)
    return pl.pallas_call(
        functools.partial(_attn_b_kernel, group=group),
        grid=(B, n_kv),
        in_specs=[pl.BlockSpec(memory_space=pltpu.SMEM),
                  pl.BlockSpec((None, n_t, S, PAIR), lambda b, g: (b, g, 0, 0)),
                  kv_spec, kv_spec],
        out_specs=pl.BlockSpec((None, S, n_t * PAIR), lambda b, g: (b, 0, g)),
        out_shape=jax.ShapeDtypeStruct((B, S, n_p * PAIR), F32),
        scratch_shapes=[pltpu.VMEM(((tk - bq) // WINDOW_B + 1, bq, tk), F32)],
        compiler_params=pltpu.CompilerParams(
            dimension_semantics=("parallel", "parallel"), vmem_limit_bytes=VMEM_LIMIT),
        name="attn_window",
    )(sink, qb, kb, vb)


def _rms(x, g):
    ms = jnp.mean(x * x, axis=-1, keepdims=True)
    return x * lax.rsqrt(ms + EPS) * g


def _mlp_kernel(oa_ref, ob_ref, x_ref, ga_ref, gb_ref, wo_ref, gm_ref, wu_ref, wd_ref, y_ref,
                *, ff_chunk):
    wa = oa_ref.shape[1]
    na = _rms(oa_ref[...], ga_ref[...]).astype(BF16)
    nb = _rms(ob_ref[...], gb_ref[...]).astype(BF16)
    h = x_ref[...] + _dot(na, wo_ref[:wa, :]) + _dot(nb, wo_ref[wa:, :])
    hn = _rms(h, gm_ref[...]).astype(BF16)
    acc = None
    for c in range(0, wu_ref.shape[1], ff_chunk):
        u = jnp.maximum(_dot(hn, wu_ref[:, c:c + ff_chunk]), 0.0)
        d = _dot((u * u).astype(BF16), wd_ref[c:c + ff_chunk, :])
        acc = d if acc is None else acc + d
    y_ref[...] = h + acc


def _mlp_call(oa, ob, x, ga, gb, wo, gm, wu, wd, *, tm, ff_chunk):
    B, S, D = x.shape
    const = lambda b, i: (0, 0)

    def resident(a):
        return pl.BlockSpec(a.shape, const, pipeline_mode=pl.Buffered(1))

    def rows(a):
        return pl.BlockSpec((None, tm, a.shape[2]), lambda b, i: (b, i, 0))

    return pl.pallas_call(
        functools.partial(_mlp_kernel, ff_chunk=ff_chunk),
        grid=(B, S // tm),
        in_specs=[rows(oa), rows(ob), rows(x), resident(ga), resident(gb), resident(wo),
                  resident(gm), resident(wu), resident(wd)],
        out_specs=rows(x),
        out_shape=jax.ShapeDtypeStruct((B, S, D), F32),
        compiler_params=pltpu.CompilerParams(
            dimension_semantics=("parallel", "parallel"), vmem_limit_bytes=VMEM_LIMIT),
        name="out_mlp",
    )(oa, ob, x, ga, gb, wo, gm, wu, wd)


def _rope_tables(S):
    inv = ROPE_THETA ** (-jnp.arange(ROPE_HALF, dtype=F32) * 2.0 / ROPE_DIM)
    ang = jnp.arange(S, dtype=F32)[:, None] * inv[None, :]
    cos, sin = jnp.cos(ang), jnp.sin(ang)
    ones = jnp.ones((S, HEAD_DIM - ROPE_DIM), F32)
    zeros = jnp.zeros((S, HEAD_DIM - ROPE_DIM), F32)
    z8 = jnp.zeros((S, ROPE_HALF), F32)
    head_cos = jnp.concatenate([cos, cos, ones], axis=1)
    head_up = jnp.concatenate([-sin, z8, zeros], axis=1)
    head_dn = jnp.concatenate([z8, sin, zeros], axis=1)
    rep = LANES // HEAD_DIM
    return tuple(jnp.tile(t, (1, rep)) for t in (head_cos, head_up, head_dn))


TILES = dict(tm_proj=512, tm_mlp=512, ff_chunk=512, group_a=4, group_b=2)


def _layer(x, p):
    B, S, D = x.shape
    width_a, width_b, kv_b = p["width_a"], p["width_b"], p["kv_b"]
    cos, sin_up, sin_dn = _rope_tables(S)
    outs = _proj_call(x, p["norm_attn"], p["w_in"], p["gcat"], p["bd"], cos, sin_up, sin_dn,
                      width_a=width_a, width_b=width_b, kv_b=kv_b, tm=TILES["tm_proj"])
    qa1, ka1, va1, qa4, ka4, va4, qa16, ka16, va16, qb, kb, vb = outs
    oa = _attn_a_call((qa1, ka1, va1, qa4, ka4, va4, qa16, ka16, va16), B, S, width_a,
                      group=TILES["group_a"])
    ob = _attn_b_call(p["sink"], qb, kb, vb, group=TILES["group_b"])
    return _mlp_call(oa, ob, x, p["out_norm_a"], p["out_norm_b"], p["w_o"], p["norm_mlp"],
                     p["w_up"], p["w_down"], tm=TILES["tm_mlp"], ff_chunk=TILES["ff_chunk"])


def _prepare(norm_attn, w_in, q_norm_a, k_norm_a, q_norm_b, k_norm_b, sink_b, out_norm_a,
             out_norm_b, w_o, norm_mlp, w_up, w_down):
    width_a = out_norm_a.shape[0]
    width_b = out_norm_b.shape[0]
    kv_b = (w_in.shape[1] - 3 * width_a - width_b) // 2
    o_qa, o_ka, o_va = 0, width_a, 2 * width_a
    o_qb = 3 * width_a
    o_kb = o_qb + width_b
    o_vb = o_kb + kv_b
    w = jnp.concatenate([w_in[:, o_qa:o_qa + width_a], w_in[:, o_ka:o_ka + width_a],
                         w_in[:, o_qb:o_qb + width_b], w_in[:, o_kb:o_kb + kv_b],
                         w_in[:, o_va:o_va + width_a], w_in[:, o_vb:o_vb + kv_b]], axis=1)
    scale = HEAD_DIM ** -0.5
    gcat = jnp.concatenate([jnp.tile(q_norm_a, width_a // HEAD_DIM) * scale,
                            jnp.tile(k_norm_a, width_a // HEAD_DIM),
                            jnp.tile(q_norm_b, width_b // HEAD_DIM) * scale,
                            jnp.tile(k_norm_b, kv_b // HEAD_DIM)])[None, :]
    n = 2 * LANES
    bd = (jnp.arange(n)[:, None] // HEAD_DIM == jnp.arange(n)[None, :] // HEAD_DIM).astype(BF16)
    return dict(width_a=width_a, width_b=width_b, kv_b=kv_b,
                norm_attn=norm_attn[None, :], w_in=w.astype(BF16), gcat=gcat, bd=bd,
                sink=sink_b, out_norm_a=out_norm_a[None, :], out_norm_b=out_norm_b[None, :],
                w_o=w_o.astype(BF16), norm_mlp=norm_mlp[None, :],
                w_up=w_up.astype(BF16), w_down=w_down.astype(BF16))


def kernel(x_prompt, x_sample, norm_attn, w_in, q_norm_a, k_norm_a, q_norm_b, k_norm_b, sink_b,
           out_norm_a, out_norm_b, w_o, norm_mlp, w_up, w_down):
    yp, ys = x_prompt, x_sample
    for l in range(w_in.shape[0]):
        p = _prepare(norm_attn[l], w_in[l], q_norm_a[l], k_norm_a[l], q_norm_b[l], k_norm_b[l],
                     sink_b[l], out_norm_a[l], out_norm_b[l], w_o[l], norm_mlp[l], w_up[l],
                     w_down[l])
        yp = _layer(yp, p)
        ys = _layer(ys, p)
    return (yp, ys)
```

```python
import functools

import jax
import jax.numpy as jnp
from jax import lax
from jax.experimental import pallas as pl
from jax.experimental.pallas import tpu as pltpu

HEAD_DIM = 64
PAIR = 2 * HEAD_DIM
DILATED_BRANCHES = ((128, 1), (512, 4), (2048, 16))
WINDOW_B = 128
ROPE_THETA = 500000.0
ROPE_DIM = HEAD_DIM // 4
ROPE_HALF = ROPE_DIM // 2
EPS = 1e-6
NEG_BIG = -1e30
LOG2E = 1.4426950408889634
LANES = 128
V7X_VMEM_BYTES = 64 * 1024 * 1024
VMEM_LIMIT = V7X_VMEM_BYTES - 8 * 1024 * 1024

F32 = jnp.float32
BF16 = jnp.bfloat16


def _dot(a, b):
    return jnp.dot(a, b, preferred_element_type=F32)


def _dot_nt(a, b):
    return lax.dot_general(a, b, (((1,), (1,)), ((), ())), preferred_element_type=F32)


def _proj_kernel(x_ref, g_ref, w_ref, gcat_ref, bd_ref, c_ref, s1_ref, s2_ref,
                 qa1, ka1, va1, qa4, ka4, va4, qa16, ka16, va16, qb, kb, vb,
                 stage_ref, stage4_ref, *, width_a, width_b, kv_b):
    tm = x_ref.shape[0]
    x = x_ref[...]
    ms = jnp.mean(x * x, axis=-1, keepdims=True)
    xn = (x * lax.rsqrt(ms + EPS) * g_ref[...]).astype(BF16)
    cos = c_ref[...]
    sin_up = s1_ref[...]
    sin_dn = s2_ref[...]
    bd = bd_ref[...]

    def head_norm_rope(col0, width):
        h = _dot(xn, w_ref[:, col0:col0 + width])
        sq = (h * h).astype(BF16)
        blocks = []
        sub = min(width, bd.shape[0])
        for c in range(0, width, sub):
            msq = _dot(sq[:, c:c + sub], bd[:sub, :sub]) * (1.0 / HEAD_DIM)
            y = h[:, c:c + sub] * lax.rsqrt(msq + EPS) * gcat_ref[:, col0 + c:col0 + c + sub]
            for b in range(0, sub, LANES):
                yb = y[:, b:b + LANES]
                up = pltpu.roll(yb, LANES - ROPE_HALF, 1)
                dn = pltpu.roll(yb, ROPE_HALF, 1)
                blocks.append(yb * cos + up * sin_up + dn * sin_dn)
        return blocks

    def write_dilated(blocks, outs, slab0):
        o1, o4, o16 = outs
        n4, n16 = tm // 4, tm // 16
        for p, blk in enumerate(blocks):
            slab = slab0 + p
            o1[p, 0] = blk.astype(BF16)
            stage_ref[slab] = blk
            for r4 in range(4):
                cls = stage_ref[slab, pl.ds(r4, n4, stride=4), :]
                o4[p, r4] = cls.astype(BF16)
                stage4_ref[slab, r4 * n4:(r4 + 1) * n4, :] = cls
            for r4 in range(4):
                for c in range(4):
                    sub = stage4_ref[slab, pl.ds(r4 * n4 + c, n16, stride=4), :]
                    o16[p, 4 * c + r4] = sub.astype(BF16)

    n_pa = width_a // LANES
    c0 = 0
    write_dilated(head_norm_rope(c0, width_a), (qa1, qa4, qa16), 0)
    c0 += width_a
    write_dilated(head_norm_rope(c0, width_a), (ka1, ka4, ka16), n_pa)
    c0 += width_a
    for p, blk in enumerate(head_norm_rope(c0, width_b)):
        qb[p] = blk.astype(BF16)
    c0 += width_b
    (kblk,) = head_norm_rope(c0, kv_b)
    c0 += kv_b
    hv = _dot(xn, w_ref[:, c0:c0 + width_a])
    write_dilated([hv[:, b:b + LANES] for b in range(0, width_a, LANES)], (va1, va4, va16),
                  2 * n_pa)
    c0 += width_a
    vblk = _dot(xn, w_ref[:, c0:c0 + kv_b])

    lo = lax.broadcasted_iota(jnp.int32, (tm, LANES), 1) < HEAD_DIM
    for blk, out in ((kblk, kb), (vblk, vb)):
        sw = pltpu.roll(blk, HEAD_DIM, 1)
        out[0] = jnp.where(lo, blk, sw).astype(BF16)
        out[1] = jnp.where(lo, sw, blk).astype(BF16)


def _proj_call(x, g, w, gcat, bd, cos, sin_up, sin_dn, *, width_a, width_b, kv_b, tm):
    B, S, D = x.shape
    n_pa = width_a // PAIR
    n_pb = width_b // PAIR
    n_kv = kv_b // HEAD_DIM
    cols = w.shape[1]

    def dil_shape(d):
        return jax.ShapeDtypeStruct((B, n_pa, d, S // d, PAIR), BF16)

    def dil_spec(d):
        return pl.BlockSpec((None, n_pa, d, tm // d, PAIR), lambda b, i: (b, 0, 0, i, 0))

    const = lambda b, i: (0, 0)
    out_shape = []
    out_specs = []
    for d in (1, 4, 16):
        out_shape += [dil_shape(d)] * 3
        out_specs += [dil_spec(d)] * 3
    out_shape += [jax.ShapeDtypeStruct((B, n_pb, S, PAIR), BF16),
                  jax.ShapeDtypeStruct((B, n_kv, S, PAIR), BF16),
                  jax.ShapeDtypeStruct((B, n_kv, S, PAIR), BF16)]
    out_specs += [pl.BlockSpec((None, n_pb, tm, PAIR), lambda b, i: (b, 0, i, 0)),
                  pl.BlockSpec((None, n_kv, tm, PAIR), lambda b, i: (b, 0, i, 0)),
                  pl.BlockSpec((None, n_kv, tm, PAIR), lambda b, i: (b, 0, i, 0))]
    tab_spec = pl.BlockSpec((tm, LANES), lambda b, i: (i, 0))
    return pl.pallas_call(
        functools.partial(_proj_kernel, width_a=width_a, width_b=width_b, kv_b=kv_b),
        grid=(B, S // tm),
        in_specs=[pl.BlockSpec((None, tm, D), lambda b, i: (b, i, 0)),
                  pl.BlockSpec((1, D), const),
                  pl.BlockSpec((D, cols), const, pipeline_mode=pl.Buffered(1)),
                  pl.BlockSpec((1, gcat.shape[1]), const),
                  pl.BlockSpec(bd.shape, const),
                  tab_spec, tab_spec, tab_spec],
        out_specs=out_specs,
        out_shape=out_shape,
        scratch_shapes=[pltpu.VMEM((3 * n_pa, tm, LANES), F32)] * 2,
        compiler_params=pltpu.CompilerParams(
            dimension_semantics=("parallel", "parallel"), vmem_limit_bytes=VMEM_LIMIT),
        name="proj_qkv",
    )(x, g, w, gcat, bd, cos, sin_up, sin_dn)


def _attend(blocks, sinks, finish):
    bq = blocks[0][0][0].shape[0]
    lo = lax.broadcasted_iota(jnp.int32, (bq, PAIR), 1) < HEAD_DIM
    zero = jnp.zeros((bq, PAIR), BF16)

    def scores(i):
        q_tiles, k, _, _ = blocks[i]
        rows = []
        for q in q_tiles:
            rows.append(jnp.where(lo, q, zero))
            rows.append(jnp.where(lo, zero, q))
        return _dot_nt(jnp.concatenate(rows, axis=0), k)

    s_next = scores(0)
    for i, (q_tiles, _, v, bias) in enumerate(blocks):
        s = s_next
        if i + 1 < len(blocks):
            s_next = scores(i + 1)
        ps, ms = [], []
        for h in range(2 * len(q_tiles)):
            sh = s[h * bq:(h + 1) * bq] + bias
            m = jnp.max(sh, axis=-1, keepdims=True)
            if sinks is not None:
                m = jnp.maximum(m, sinks[h])
            ps.append(jnp.exp2(sh - m).astype(BF16))
            ms.append(m)
        pv = _dot(jnp.concatenate(ps, axis=0),
                  jnp.concatenate([v, jnp.ones_like(v)], axis=1))
        res = []
        for t in range(len(q_tiles)):
            a, b = 2 * t, 2 * t + 1
            pa, pb = pv[a * bq:(a + 1) * bq], pv[b * bq:(b + 1) * bq]
            la, lb = pa[:, PAIR:], pb[:, PAIR:]
            if sinks is not None:
                la = la + jnp.exp2(sinks[a] - ms[a])
                lb = lb + jnp.exp2(sinks[b] - ms[b])
            res.append((jnp.where(lo, pa[:, :PAIR], pb[:, :PAIR]),
                        jnp.where(lo, ms[a], ms[b]), jnp.where(lo, la, lb)))
        finish(i, res)


def _band_mask(bq, tk, offset, half_window):
    r = lax.broadcasted_iota(jnp.int32, (bq, tk), 0)
    c = lax.broadcasted_iota(jnp.int32, (bq, tk), 1)
    return jnp.abs(c - r + offset) <= half_window


def _branch_cfg(S, window, d):
    hw = window // (2 * d)
    seg = S // d
    bq = min(2 * hw, seg)
    tk = min(bq + 2 * hw, seg)
    return hw, seg, bq, tk


def _fill_bias(bias_ref, bq, tk, hw):
    for n in range(bias_ref.shape[0]):
        bias_ref[n] = jnp.where(_band_mask(bq, tk, -n * hw, hw), 0.0, NEG_BIG)


def _attn_a_kernel(q1, k1, v1, q4, k4, v4, q16, k16, v16, o_ref, acc_ref, m_ref, l_ref, tmp_ref,
                   *bias_refs, group):
    S = o_ref.shape[0]
    S4 = S // 4
    state = (acc_ref, m_ref, l_ref)
    refs = {1: (q1, k1, v1), 4: (q4, k4, v4), 16: (q16, k16, v16)}
    assert [d for _, d in DILATED_BRANCHES] == [1, 4, 16]
    order = (1, 2, 0)
    for bi in order:
        window, d = DILATED_BRANCHES[bi]
        q_ref, k_ref, v_ref = refs[d]
        hw, seg, bq, tk = _branch_cfg(S, window, d)
        _fill_bias(bias_refs[bi], bq, tk, hw)
        n4 = bq // 4

        def step(j, carry, d=d, hw=hw, seg=seg, bq=bq, tk=tk, n4=n4,
                 q_ref=q_ref, k_ref=k_ref, v_ref=v_ref, bias_ref=bias_refs[bi]):
            blocks, starts = [], []
            for g in range(group):
                i0 = pl.multiple_of((j * group + g) * bq, bq)
                r = i0 // seg
                m0 = i0 - r * seg
                back = m0 - jnp.clip(m0 - hw, 0, seg - tk)
                ks = pl.multiple_of(i0 - back, hw)
                blocks.append(([q_ref[pl.ds(i0, bq), :]], k_ref[pl.ds(ks, tk), :],
                               v_ref[pl.ds(ks, tk), :], bias_ref[back // hw]))
                starts.append((i0, r, m0))
                if d == 1:
                    for a, ref in enumerate(state):
                        for r4 in range(4):
                            piece = ref[pl.ds(r4 * S4 + i0 // 4, n4), :]
                            tmp_ref[3 * g + a, pl.ds(r4, n4, stride=4), :] = piece

            def finish(g, res):
                i0, r, m0 = starts[g]
                ((acc, m, l),) = res
                if d == 4:
                    rows = pl.ds(i0, bq)
                    acc_ref[rows, :] = acc
                    m_ref[rows, :] = m
                    l_ref[rows, :] = l
                    return
                if d == 16:
                    rows = pl.ds((r % 4) * S4 + 4 * m0 + r // 4, bq, stride=4)
                    old = [ref[rows, :] for ref in state]
                else:
                    old = [tmp_ref[3 * g + a] for a in range(3)]
                acc_old, m_old, l_old = old
                m_new = jnp.maximum(m_old, m)
                a_old = jnp.exp2(m_old - m_new)
                a_cur = jnp.exp2(m - m_new)
                acc = a_old * acc_old + a_cur * acc
                l = a_old * l_old + a_cur * l
                if d == 16:
                    acc_ref[rows, :] = acc
                    m_ref[rows, :] = m_new
                    l_ref[rows, :] = l
                else:
                    o_ref[pl.ds(i0, bq), :] = acc / l

            _attend(blocks, None, finish)
            return carry

        lax.fori_loop(0, S // (bq * group), step, 0)


def _attn_a_call(qkv, B, S, width_a, *, group):
    n_p = width_a // PAIR
    flat = [t.reshape(B, n_p, S, PAIR) for t in qkv]
    spec = pl.BlockSpec((None, None, S, PAIR), lambda b, p: (b, p, 0, 0))
    bias_shapes = []
    for window, d in DILATED_BRANCHES:
        hw, seg, bq, tk = _branch_cfg(S, window, d)
        bias_shapes.append(pltpu.VMEM(((tk - bq) // hw + 1, bq, tk), F32))
    return pl.pallas_call(
        functools.partial(_attn_a_kernel, group=group),
        grid=(B, n_p),
        in_specs=[spec] * 9,
        out_specs=pl.BlockSpec((None, S, PAIR), lambda b, p: (b, 0, p)),
        out_shape=jax.ShapeDtypeStruct((B, S, width_a), F32),
        scratch_shapes=[pltpu.VMEM((S, PAIR), F32)] * 3
        + [pltpu.VMEM((3 * group, _branch_cfg(S, *DILATED_BRANCHES[0])[2], PAIR), F32)]
        + bias_shapes,
        compiler_params=pltpu.CompilerParams(
            dimension_semantics=("parallel", "parallel"), vmem_limit_bytes=VMEM_LIMIT),
        name="attn_dilated",
    )(*flat)


def _attn_b_kernel(sink_ref, q_ref, k_ref, v_ref, o_ref, bias_ref, *, group):
    n_t, S, _ = q_ref.shape
    kv_head = pl.program_id(1)
    bq = WINDOW_B
    tk = min(3 * bq, S)
    sinks = [sink_ref[kv_head * 2 * n_t + h] for h in range(2 * n_t)]
    _fill_bias(bias_ref, bq, tk, WINDOW_B)

    def step(j, carry):
        blocks, starts = [], []
        for g in range(group):
            i0 = pl.multiple_of((j * group + g) * bq, bq)
            ks = pl.multiple_of(jnp.clip(i0 - WINDOW_B, 0, S - tk), WINDOW_B)
            blocks.append(([q_ref[t, pl.ds(i0, bq), :] for t in range(n_t)],
                           k_ref[pl.ds(ks, tk), :], v_ref[pl.ds(ks, tk), :],
                           bias_ref[(i0 - ks) // WINDOW_B]))
            starts.append(i0)
        def finish(g, res):
            for t, (acc, m, l) in enumerate(res):
                o_ref[pl.ds(starts[g], bq), t * PAIR:(t + 1) * PAIR] = acc / l

        _attend(blocks, sinks, finish)
        return carry

    lax.fori_loop(0, S // (bq * group), step, 0)


def _attn_b_call(sink, qb, kb, vb, *, group):
    B, n_p, S, _ = qb.shape
    n_kv = kb.shape[1]
    n_t = n_p // n_kv
    kv_spec = pl.BlockSpec((None, None, S, PAIR), lambda b, g: (b, g, 0, 0))
    bq = WINDOW_B
    tk = min(3 * bq, S)
    return pl.pallas_call(
        functools.partial(_attn_b_kernel, group=group),
        grid=(B, n_kv),
        in_specs=[pl.BlockSpec(memory_space=pltpu.SMEM),
                  pl.BlockSpec((None, n_t, S, PAIR), lambda b, g: (b, g, 0, 0)),
                  kv_spec, kv_spec],
        out_specs=pl.BlockSpec((None, S, n_t * PAIR), lambda b, g: (b, 0, g)),
        out_shape=jax.ShapeDtypeStruct((B, S, n_p * PAIR), F32),
        scratch_shapes=[pltpu.VMEM(((tk - bq) // WINDOW_B + 1, bq, tk), F32)],
        compiler_params=pltpu.CompilerParams(
            dimension_semantics=("parallel", "parallel"), vmem_limit_bytes=VMEM_LIMIT),
        name="attn_window",
    )(sink, qb, kb, vb)


def _rms(x, g):
    ms = jnp.mean(x * x, axis=-1, keepdims=True)
    return x * lax.rsqrt(ms + EPS) * g


def _mlp_kernel(oa_ref, ob_ref, x_ref, ga_ref, gb_ref, wo_ref, gm_ref, wu_ref, wd_ref, y_ref,
                *, ff_chunk):
    wa = oa_ref.shape[1]
    na = _rms(oa_ref[...], ga_ref[...]).astype(BF16)
    nb = _rms(ob_ref[...], gb_ref[...]).astype(BF16)
    h = x_ref[...] + _dot(na, wo_ref[:wa, :]) + _dot(nb, wo_ref[wa:, :])
    hn = _rms(h, gm_ref[...]).astype(BF16)
    acc = None
    for c in range(0, wu_ref.shape[1], ff_chunk):
        u = jnp.maximum(_dot(hn, wu_ref[:, c:c + ff_chunk]), 0.0)
        d = _dot((u * u).astype(BF16), wd_ref[c:c + ff_chunk, :])
        acc = d if acc is None else acc + d
    y_ref[...] = h + acc


def _mlp_call(oa, ob, x, ga, gb, wo, gm, wu, wd, *, tm, ff_chunk):
    B, S, D = x.shape
    const = lambda b, i: (0, 0)

    def resident(a):
        return pl.BlockSpec(a.shape, const, pipeline_mode=pl.Buffered(1))

    def rows(a):
        return pl.BlockSpec((None, tm, a.shape[2]), lambda b, i: (b, i, 0))

    return pl.pallas_call(
        functools.partial(_mlp_kernel, ff_chunk=ff_chunk),
        grid=(B, S // tm),
        in_specs=[rows(oa), rows(ob), rows(x), resident(ga), resident(gb), resident(wo),
                  resident(gm), resident(wu), resident(wd)],
        out_specs=rows(x),
        out_shape=jax.ShapeDtypeStruct((B, S, D), F32),
        compiler_params=pltpu.CompilerParams(
            dimension_semantics=("parallel", "parallel"), vmem_limit_bytes=VMEM_LIMIT),
        name="out_mlp",
    )(oa, ob, x, ga, gb, wo, gm, wu, wd)


def _rope_tables(S):
    inv = ROPE_THETA ** (-jnp.arange(ROPE_HALF, dtype=F32) * 2.0 / ROPE_DIM)
    ang = jnp.arange(S, dtype=F32)[:, None] * inv[None, :]
    cos, sin = jnp.cos(ang), jnp.sin(ang)
    ones = jnp.ones((S, HEAD_DIM - ROPE_DIM), F32)
    zeros = jnp.zeros((S, HEAD_DIM - ROPE_DIM), F32)
    z8 = jnp.zeros((S, ROPE_HALF), F32)
    head_cos = jnp.concatenate([cos, cos, ones], axis=1)
    head_up = jnp.concatenate([-sin, z8, zeros], axis=1)
    head_dn = jnp.concatenate([z8, sin, zeros], axis=1)
    rep = LANES // HEAD_DIM
    return tuple(jnp.tile(t, (1, rep)) for t in (head_cos, head_up, head_dn))


TILES = dict(tm_proj=512, tm_mlp=512, ff_chunk=512, group_a=8, group_b=4)


def _layer(x, p):
    B, S, D = x.shape
    width_a, width_b, kv_b = p["width_a"], p["width_b"], p["kv_b"]
    cos, sin_up, sin_dn = _rope_tables(S)
    outs = _proj_call(x, p["norm_attn"], p["w_in"], p["gcat"], p["bd"], cos, sin_up, sin_dn,
                      width_a=width_a, width_b=width_b, kv_b=kv_b, tm=TILES["tm_proj"])
    qa1, ka1, va1, qa4, ka4, va4, qa16, ka16, va16, qb, kb, vb = outs
    oa = _attn_a_call((qa1, ka1, va1, qa4, ka4, va4, qa16, ka16, va16), B, S, width_a,
                      group=TILES["group_a"])
    ob = _attn_b_call(p["sink"], qb, kb, vb, group=TILES["group_b"])
    return _mlp_call(oa, ob, x, p["out_norm_a"], p["out_norm_b"], p["w_o"], p["norm_mlp"],
                     p["w_up"], p["w_down"], tm=TILES["tm_mlp"], ff_chunk=TILES["ff_chunk"])


def _prepare(norm_attn, w_in, q_norm_a, k_norm_a, q_norm_b, k_norm_b, sink_b, out_norm_a,
             out_norm_b, w_o, norm_mlp, w_up, w_down):
    width_a = out_norm_a.shape[0]
    width_b = out_norm_b.shape[0]
    kv_b = (w_in.shape[1] - 3 * width_a - width_b) // 2
    o_qa, o_ka, o_va = 0, width_a, 2 * width_a
    o_qb = 3 * width_a
    o_kb = o_qb + width_b
    o_vb = o_kb + kv_b
    w = jnp.concatenate([w_in[:, o_qa:o_qa + width_a], w_in[:, o_ka:o_ka + width_a],
                         w_in[:, o_qb:o_qb + width_b], w_in[:, o_kb:o_kb + kv_b],
                         w_in[:, o_va:o_va + width_a], w_in[:, o_vb:o_vb + kv_b]], axis=1)
    scale = HEAD_DIM ** -0.5 * LOG2E
    gcat = jnp.concatenate([jnp.tile(q_norm_a, width_a // HEAD_DIM) * scale,
                            jnp.tile(k_norm_a, width_a // HEAD_DIM),
                            jnp.tile(q_norm_b, width_b // HEAD_DIM) * scale,
                            jnp.tile(k_norm_b, kv_b // HEAD_DIM)])[None, :]
    n = 2 * LANES
    bd = (jnp.arange(n)[:, None] // HEAD_DIM == jnp.arange(n)[None, :] // HEAD_DIM).astype(BF16)
    return dict(width_a=width_a, width_b=width_b, kv_b=kv_b,
                norm_attn=norm_attn[None, :], w_in=w.astype(BF16), gcat=gcat, bd=bd,
                sink=sink_b * LOG2E, out_norm_a=out_norm_a[None, :], out_norm_b=out_norm_b[None, :],
                w_o=w_o.astype(BF16), norm_mlp=norm_mlp[None, :],
                w_up=w_up.astype(BF16), w_down=w_down.astype(BF16))


def kernel(x_prompt, x_sample, norm_attn, w_in, q_norm_a, k_norm_a, q_norm_b, k_norm_b, sink_b,
           out_norm_a, out_norm_b, w_o, norm_mlp, w_up, w_down):
    yp, ys = x_prompt, x_sample
    for l in range(w_in.shape[0]):
        p = _prepare(norm_attn[l], w_in[l], q_norm_a[l], k_norm_a[l], q_norm_b[l], k_norm_b[l],
                     sink_b[l], out_norm_a[l], out_norm_b[l], w_o[l], norm_mlp[l], w_up[l],
                     w_down[l])
        yp = _layer(yp, p)
        ys = _layer(ys, p)
    return (yp, ys)
```

```python
import functools

import jax
import jax.numpy as jnp
from jax import lax
from jax.experimental import pallas as pl
from jax.experimental.pallas import tpu as pltpu

HEAD_DIM = 64
PAIR = 2 * HEAD_DIM
DILATED_BRANCHES = ((128, 1), (512, 4), (2048, 16))
WINDOW_B = 128
ROPE_THETA = 500000.0
ROPE_DIM = HEAD_DIM // 4
ROPE_HALF = ROPE_DIM // 2
EPS = 1e-6
NEG_BIG = -1e30
LOG2E = 1.4426950408889634
LANES = 128
V7X_VMEM_BYTES = 64 * 1024 * 1024
VMEM_LIMIT = V7X_VMEM_BYTES - 8 * 1024 * 1024

F32 = jnp.float32
BF16 = jnp.bfloat16


def _dot(a, b):
    return jnp.dot(a, b, preferred_element_type=F32)


def _dot_nt(a, b):
    return lax.dot_general(a, b, (((1,), (1,)), ((), ())), preferred_element_type=F32)


def _proj_kernel(x_ref, g_ref, w_ref, gcat_ref, bd_ref, c_ref, s1_ref, s2_ref,
                 qa1, ka1, va1, qa4, ka4, va4, qa16, ka16, va16, qb, kb, vb,
                 stage_ref, stage4_ref, *, width_a, width_b, kv_b):
    tm = x_ref.shape[0]
    x = x_ref[...]
    ms = jnp.mean(x * x, axis=-1, keepdims=True)
    xn = (x * lax.rsqrt(ms + EPS) * g_ref[...]).astype(BF16)
    cos = c_ref[...]
    sin_up = s1_ref[...]
    sin_dn = s2_ref[...]
    bd = bd_ref[...]

    def head_norm_rope(h, col0):
        width = h.shape[1]
        sq = (h * h).astype(BF16)
        blocks = []
        sub = min(width, bd.shape[0])
        for c in range(0, width, sub):
            msq = _dot(sq[:, c:c + sub], bd[:sub, :sub]) * (1.0 / HEAD_DIM)
            y = h[:, c:c + sub] * lax.rsqrt(msq + EPS) * gcat_ref[:, col0 + c:col0 + c + sub]
            for b in range(0, sub, LANES):
                yb = y[:, b:b + LANES]
                up = pltpu.roll(yb, LANES - ROPE_HALF, 1)
                dn = pltpu.roll(yb, ROPE_HALF, 1)
                blocks.append(yb * cos + up * sin_up + dn * sin_dn)
        return blocks

    def write_dilated(blocks, outs, slab0):
        o1, o4, o16 = outs
        n4, n16 = tm // 4, tm // 16
        for p, blk in enumerate(blocks):
            slab = slab0 + p
            o1[p, 0] = blk.astype(BF16)
            stage_ref[slab] = blk
            for r4 in range(4):
                cls = stage_ref[slab, pl.ds(r4, n4, stride=4), :]
                o4[p, r4] = cls.astype(BF16)
                stage4_ref[slab, r4 * n4:(r4 + 1) * n4, :] = cls
            for r4 in range(4):
                for c in range(4):
                    sub = stage4_ref[slab, pl.ds(r4 * n4 + c, n16, stride=4), :]
                    o16[p, 4 * c + r4] = sub.astype(BF16)

    def write_both_halves(blk, out):
        lo = lax.broadcasted_iota(jnp.int32, (tm, LANES), 1) < HEAD_DIM
        sw = pltpu.roll(blk, HEAD_DIM, 1)
        out[0] = jnp.where(lo, blk, sw).astype(BF16)
        out[1] = jnp.where(lo, sw, blk).astype(BF16)

    def lane_blocks(h):
        return [h[:, b:b + LANES] for b in range(0, h.shape[1], LANES)]

    def write_qb(blocks):
        for p, blk in enumerate(blocks):
            qb[p] = blk.astype(BF16)

    n_pa = width_a // LANES
    chunks = [
        (width_a, lambda h, c: write_dilated(head_norm_rope(h, c), (qa1, qa4, qa16), 0)),
        (width_a, lambda h, c: write_dilated(head_norm_rope(h, c), (ka1, ka4, ka16), n_pa)),
        (width_b, lambda h, c: write_qb(head_norm_rope(h, c))),
        (kv_b, lambda h, c: write_both_halves(head_norm_rope(h, c)[0], kb)),
        (width_a, lambda h, c: write_dilated(lane_blocks(h), (va1, va4, va16), 2 * n_pa)),
        (kv_b, lambda h, c: write_both_halves(h, vb)),
    ]
    starts = [sum(w for w, _ in chunks[:i]) for i in range(len(chunks))]

    def project(i):
        return _dot(xn, w_ref[:, starts[i]:starts[i] + chunks[i][0]])

    h_next = project(0)
    for i, (_, consume) in enumerate(chunks):
        h = h_next
        if i + 1 < len(chunks):
            h_next = project(i + 1)
        consume(h, starts[i])


def _proj_call(x, g, w, gcat, bd, cos, sin_up, sin_dn, *, width_a, width_b, kv_b, tm):
    B, S, D = x.shape
    n_pa = width_a // PAIR
    n_pb = width_b // PAIR
    n_kv = kv_b // HEAD_DIM
    cols = w.shape[1]

    def dil_shape(d):
        return jax.ShapeDtypeStruct((B, n_pa, d, S // d, PAIR), BF16)

    def dil_spec(d):
        return pl.BlockSpec((None, n_pa, d, tm // d, PAIR), lambda b, i: (b, 0, 0, i, 0))

    const = lambda b, i: (0, 0)
    out_shape = []
    out_specs = []
    for d in (1, 4, 16):
        out_shape += [dil_shape(d)] * 3
        out_specs += [dil_spec(d)] * 3
    out_shape += [jax.ShapeDtypeStruct((B, n_pb, S, PAIR), BF16),
                  jax.ShapeDtypeStruct((B, n_kv, S, PAIR), BF16),
                  jax.ShapeDtypeStruct((B, n_kv, S, PAIR), BF16)]
    out_specs += [pl.BlockSpec((None, n_pb, tm, PAIR), lambda b, i: (b, 0, i, 0)),
                  pl.BlockSpec((None, n_kv, tm, PAIR), lambda b, i: (b, 0, i, 0)),
                  pl.BlockSpec((None, n_kv, tm, PAIR), lambda b, i: (b, 0, i, 0))]
    tab_spec = pl.BlockSpec((tm, LANES), lambda b, i: (i, 0))
    return pl.pallas_call(
        functools.partial(_proj_kernel, width_a=width_a, width_b=width_b, kv_b=kv_b),
        grid=(B, S // tm),
        in_specs=[pl.BlockSpec((None, tm, D), lambda b, i: (b, i, 0)),
                  pl.BlockSpec((1, D), const),
                  pl.BlockSpec((D, cols), const, pipeline_mode=pl.Buffered(1)),
                  pl.BlockSpec((1, gcat.shape[1]), const),
                  pl.BlockSpec(bd.shape, const),
                  tab_spec, tab_spec, tab_spec],
        out_specs=out_specs,
        out_shape=out_shape,
        scratch_shapes=[pltpu.VMEM((3 * n_pa, tm, LANES), F32)] * 2,
        compiler_params=pltpu.CompilerParams(
            dimension_semantics=("parallel", "parallel"), vmem_limit_bytes=VMEM_LIMIT),
        name="proj_qkv",
    )(x, g, w, gcat, bd, cos, sin_up, sin_dn)


def _attend(blocks, sinks, finish):
    bq = blocks[0][0][0].shape[0]
    lo = lax.broadcasted_iota(jnp.int32, (bq, PAIR), 1) < HEAD_DIM
    zero = jnp.zeros((bq, PAIR), BF16)

    def scores(i):
        q_tiles, k, _, _ = blocks[i]
        rows = []
        for q in q_tiles:
            rows.append(jnp.where(lo, q, zero))
            rows.append(jnp.where(lo, zero, q))
        return _dot_nt(jnp.concatenate(rows, axis=0), k)

    s_next = scores(0)
    for i, (q_tiles, _, v, bias) in enumerate(blocks):
        s = s_next
        if i + 1 < len(blocks):
            s_next = scores(i + 1)
        ps, ms = [], []
        for h in range(2 * len(q_tiles)):
            sh = s[h * bq:(h + 1) * bq] + bias
            m = jnp.max(sh, axis=-1, keepdims=True)
            if sinks is not None:
                m = jnp.maximum(m, sinks[h])
            ps.append(jnp.exp2(sh - m).astype(BF16))
            ms.append(m)
        pv = _dot(jnp.concatenate(ps, axis=0),
                  jnp.concatenate([v, jnp.ones_like(v)], axis=1))
        res = []
        for t in range(len(q_tiles)):
            a, b = 2 * t, 2 * t + 1
            pa, pb = pv[a * bq:(a + 1) * bq], pv[b * bq:(b + 1) * bq]
            la, lb = pa[:, PAIR:], pb[:, PAIR:]
            if sinks is not None:
                la = la + jnp.exp2(sinks[a] - ms[a])
                lb = lb + jnp.exp2(sinks[b] - ms[b])
            res.append((jnp.where(lo, pa[:, :PAIR], pb[:, :PAIR]),
                        jnp.where(lo, ms[a], ms[b]), jnp.where(lo, la, lb)))
        finish(i, res)


def _band_mask(bq, tk, offset, half_window):
    r = lax.broadcasted_iota(jnp.int32, (bq, tk), 0)
    c = lax.broadcasted_iota(jnp.int32, (bq, tk), 1)
    return jnp.abs(c - r + offset) <= half_window


def _branch_cfg(S, window, d):
    hw = window // (2 * d)
    seg = S // d
    bq = min(2 * hw, seg)
    tk = min(bq + 2 * hw, seg)
    return hw, seg, bq, tk


def _fill_bias(bias_ref, bq, tk, hw):
    for n in range(bias_ref.shape[0]):
        bias_ref[n] = jnp.where(_band_mask(bq, tk, -n * hw, hw), 0.0, NEG_BIG)


def _attn_a_kernel(q1, k1, v1, q4, k4, v4, q16, k16, v16, o_ref, st4_ref, st16_ref, tmp_ref,
                   *bias_refs, group):
    S = o_ref.shape[0]
    S4 = S // 4
    refs = {1: (q1, k1, v1), 4: (q4, k4, v4), 16: (q16, k16, v16)}
    assert [d for _, d in DILATED_BRANCHES] == [1, 4, 16]
    cfg = {}
    for bi, (window, d) in enumerate(DILATED_BRANCHES):
        cfg[d] = _branch_cfg(S, window, d) + (bias_refs[bi],)
        hw, seg, bq, tk, bias_ref = cfg[d]
        _fill_bias(bias_ref, bq, tk, hw)
    bq = cfg[1][2]
    assert cfg[4][2] == bq and cfg[16][2] == bq
    n4 = bq // 4

    def block(d, i0):
        hw, seg, _, tk, bias_ref = cfg[d]
        q_ref, k_ref, v_ref = refs[d]
        r = i0 // seg
        m0 = i0 - r * seg
        back = m0 - jnp.clip(m0 - hw, 0, seg - tk)
        ks = pl.multiple_of(i0 - back, hw)
        return (([q_ref[pl.ds(i0, bq), :]], k_ref[pl.ds(ks, tk), :], v_ref[pl.ds(ks, tk), :],
                 bias_ref[back // hw]), (r, m0))

    def dilated_step(j, carry):
        blocks, dests = [], []
        for g in range(group):
            i0 = pl.multiple_of((j * group + g) * bq, bq)
            for d in (4, 16):
                blk, (r, m0) = block(d, i0)
                blocks.append(blk)
                if d == 4:
                    dests.append((st4_ref, pl.ds(i0, bq)))
                else:
                    dests.append((st16_ref, pl.ds((r % 4) * S4 + 4 * m0 + r // 4, bq, stride=4)))

        def finish(i, res):
            ref, rows = dests[i]
            for a, val in enumerate(res[0]):
                ref[a, rows, :] = val

        _attend(blocks, None, finish)
        return carry

    lax.fori_loop(0, S // (bq * group), dilated_step, 0)

    def local_step(j, carry):
        blocks, starts = [], []
        for g in range(group):
            i0 = pl.multiple_of((j * group + g) * bq, bq)
            blocks.append(block(1, i0)[0])
            starts.append(i0)
            for s, ref in enumerate((st4_ref, st16_ref)):
                for a in range(3):
                    for r4 in range(4):
                        piece = ref[a, pl.ds(r4 * S4 + i0 // 4, n4), :]
                        tmp_ref[6 * g + 3 * s + a, pl.ds(r4, n4, stride=4), :] = piece

        def finish(g, res):
            ((acc1, m1, l1),) = res
            acc4, m4, l4, acc16, m16, l16 = [tmp_ref[6 * g + n] for n in range(6)]
            m = jnp.maximum(jnp.maximum(m1, m4), m16)
            a1, a4, a16 = jnp.exp2(m1 - m), jnp.exp2(m4 - m), jnp.exp2(m16 - m)
            acc = a1 * acc1 + a4 * acc4 + a16 * acc16
            l = a1 * l1 + a4 * l4 + a16 * l16
            o_ref[pl.ds(starts[g], bq), :] = acc / l

        _attend(blocks, None, finish)
        return carry

    lax.fori_loop(0, S // (bq * group), local_step, 0)


def _attn_a_call(qkv, B, S, width_a, *, group):
    n_p = width_a // PAIR
    flat = [t.reshape(B, n_p, S, PAIR) for t in qkv]
    spec = pl.BlockSpec((None, None, S, PAIR), lambda b, p: (b, p, 0, 0))
    bias_shapes = []
    for window, d in DILATED_BRANCHES:
        hw, seg, bq, tk = _branch_cfg(S, window, d)
        bias_shapes.append(pltpu.VMEM(((tk - bq) // hw + 1, bq, tk), F32))
    return pl.pallas_call(
        functools.partial(_attn_a_kernel, group=group),
        grid=(B, n_p),
        in_specs=[spec] * 9,
        out_specs=pl.BlockSpec((None, S, PAIR), lambda b, p: (b, 0, p)),
        out_shape=jax.ShapeDtypeStruct((B, S, width_a), F32),
        scratch_shapes=[pltpu.VMEM((3, S, PAIR), F32)] * 2
        + [pltpu.VMEM((6 * group, _branch_cfg(S, *DILATED_BRANCHES[0])[2], PAIR), F32)]
        + bias_shapes,
        compiler_params=pltpu.CompilerParams(
            dimension_semantics=("parallel", "parallel"), vmem_limit_bytes=VMEM_LIMIT),
        name="attn_dilated",
    )(*flat)


def _attn_b_kernel(sink_ref, q_ref, k_ref, v_ref, o_ref, bias_ref, *, group):
    n_t, S, _ = q_ref.shape
    kv_head = pl.program_id(1)
    bq = WINDOW_B
    tk = min(3 * bq, S)
    sinks = [sink_ref[kv_head * 2 * n_t + h] for h in range(2 * n_t)]
    _fill_bias(bias_ref, bq, tk, WINDOW_B)

    def step(j, carry):
        blocks, starts = [], []
        for g in range(group):
            i0 = pl.multiple_of((j * group + g) * bq, bq)
            ks = pl.multiple_of(jnp.clip(i0 - WINDOW_B, 0, S - tk), WINDOW_B)
            blocks.append(([q_ref[t, pl.ds(i0, bq), :] for t in range(n_t)],
                           k_ref[pl.ds(ks, tk), :], v_ref[pl.ds(ks, tk), :],
                           bias_ref[(i0 - ks) // WINDOW_B]))
            starts.append(i0)
        def finish(g, res):
            for t, (acc, m, l) in enumerate(res):
                o_ref[pl.ds(starts[g], bq), t * PAIR:(t + 1) * PAIR] = acc / l

        _attend(blocks, sinks, finish)
        return carry

    lax.fori_loop(0, S // (bq * group), step, 0)


def _attn_b_call(sink, qb, kb, vb, *, group):
    B, n_p, S, _ = qb.shape
    n_kv = kb.shape[1]
    n_t = n_p // n_kv
    kv_spec = pl.BlockSpec((None, None, S, PAIR), lambda b, g: (b, g, 0, 0))
    bq = WINDOW_B
    tk = min(3 * bq, S)
    return pl.pallas_call(
        functools.partial(_attn_b_kernel, group=group),
        grid=(B, n_kv),
        in_specs=[pl.BlockSpec(memory_space=pltpu.SMEM),
                  pl.BlockSpec((None, n_t, S, PAIR), lambda b, g: (b, g, 0, 0)),
                  kv_spec, kv_spec],
        out_specs=pl.BlockSpec((None, S, n_t * PAIR), lambda b, g: (b, 0, g)),
        out_shape=jax.ShapeDtypeStruct((B, S, n_p * PAIR), F32),
        scratch_shapes=[pltpu.VMEM(((tk - bq) // WINDOW_B + 1, bq, tk), F32)],
        compiler_params=pltpu.CompilerParams(
            dimension_semantics=("parallel", "parallel"), vmem_limit_bytes=VMEM_LIMIT),
        name="attn_window",
    )(sink, qb, kb, vb)


def _rms(x, g):
    ms = jnp.mean(x * x, axis=-1, keepdims=True)
    return x * lax.rsqrt(ms + EPS) * g


def _mlp_kernel(oa_ref, ob_ref, x_ref, ga_ref, gb_ref, wo_ref, gm_ref, wu_ref, wd_ref, y_ref,
                *, ff_chunk):
    wa = oa_ref.shape[1]
    na = _rms(oa_ref[...], ga_ref[...]).astype(BF16)
    nb = _rms(ob_ref[...], gb_ref[...]).astype(BF16)
    h = x_ref[...] + _dot(na, wo_ref[:wa, :]) + _dot(nb, wo_ref[wa:, :])
    hn = _rms(h, gm_ref[...]).astype(BF16)
    acc = None
    for c in range(0, wu_ref.shape[1], ff_chunk):
        u = jnp.maximum(_dot(hn, wu_ref[:, c:c + ff_chunk]), 0.0)
        d = _dot((u * u).astype(BF16), wd_ref[c:c + ff_chunk, :])
        acc = d if acc is None else acc + d
    y_ref[...] = h + acc


def _mlp_call(oa, ob, x, ga, gb, wo, gm, wu, wd, *, tm, ff_chunk):
    B, S, D = x.shape
    const = lambda b, i: (0, 0)

    def resident(a):
        return pl.BlockSpec(a.shape, const, pipeline_mode=pl.Buffered(1))

    def rows(a):
        return pl.BlockSpec((None, tm, a.shape[2]), lambda b, i: (b, i, 0))

    return pl.pallas_call(
        functools.partial(_mlp_kernel, ff_chunk=ff_chunk),
        grid=(B, S // tm),
        in_specs=[rows(oa), rows(ob), rows(x), resident(ga), resident(gb), resident(wo),
                  resident(gm), resident(wu), resident(wd)],
        out_specs=rows(x),
        out_shape=jax.ShapeDtypeStruct((B, S, D), F32),
        compiler_params=pltpu.CompilerParams(
            dimension_semantics=("parallel", "parallel"), vmem_limit_bytes=VMEM_LIMIT),
        name="out_mlp",
    )(oa, ob, x, ga, gb, wo, gm, wu, wd)


def _rope_tables(S):
    inv = ROPE_THETA ** (-jnp.arange(ROPE_HALF, dtype=F32) * 2.0 / ROPE_DIM)
    ang = jnp.arange(S, dtype=F32)[:, None] * inv[None, :]
    cos, sin = jnp.cos(ang), jnp.sin(ang)
    ones = jnp.ones((S, HEAD_DIM - ROPE_DIM), F32)
    zeros = jnp.zeros((S, HEAD_DIM - ROPE_DIM), F32)
    z8 = jnp.zeros((S, ROPE_HALF), F32)
    head_cos = jnp.concatenate([cos, cos, ones], axis=1)
    head_up = jnp.concatenate([-sin, z8, zeros], axis=1)
    head_dn = jnp.concatenate([z8, sin, zeros], axis=1)
    rep = LANES // HEAD_DIM
    return tuple(jnp.tile(t, (1, rep)) for t in (head_cos, head_up, head_dn))


TILES = dict(tm_proj=512, tm_mlp=512, ff_chunk=512, group_a=8, group_b=4)


def _layer(x, p):
    B, S, D = x.shape
    width_a, width_b, kv_b = p["width_a"], p["width_b"], p["kv_b"]
    cos, sin_up, sin_dn = _rope_tables(S)
    outs = _proj_call(x, p["norm_attn"], p["w_in"], p["gcat"], p["bd"], cos, sin_up, sin_dn,
                      width_a=width_a, width_b=width_b, kv_b=kv_b, tm=TILES["tm_proj"])
    qa1, ka1, va1, qa4, ka4, va4, qa16, ka16, va16, qb, kb, vb = outs
    oa = _attn_a_call((qa1, ka1, va1, qa4, ka4, va4, qa16, ka16, va16), B, S, width_a,
                      group=TILES["group_a"])
    ob = _attn_b_call(p["sink"], qb, kb, vb, group=TILES["group_b"])
    return _mlp_call(oa, ob, x, p["out_norm_a"], p["out_norm_b"], p["w_o"], p["norm_mlp"],
                     p["w_up"], p["w_down"], tm=TILES["tm_mlp"], ff_chunk=TILES["ff_chunk"])


def _prepare(norm_attn, w_in, q_norm_a, k_norm_a, q_norm_b, k_norm_b, sink_b, out_norm_a,
             out_norm_b, w_o, norm_mlp, w_up, w_down):
    width_a = out_norm_a.shape[0]
    width_b = out_norm_b.shape[0]
    kv_b = (w_in.shape[1] - 3 * width_a - width_b) // 2
    o_qa, o_ka, o_va = 0, width_a, 2 * width_a
    o_qb = 3 * width_a
    o_kb = o_qb + width_b
    o_vb = o_kb + kv_b
    w = jnp.concatenate([w_in[:, o_qa:o_qa + width_a], w_in[:, o_ka:o_ka + width_a],
                         w_in[:, o_qb:o_qb + width_b], w_in[:, o_kb:o_kb + kv_b],
                         w_in[:, o_va:o_va + width_a], w_in[:, o_vb:o_vb + kv_b]], axis=1)
    scale = HEAD_DIM ** -0.5 * LOG2E
    gcat = jnp.concatenate([jnp.tile(q_norm_a, width_a // HEAD_DIM) * scale,
                            jnp.tile(k_norm_a, width_a // HEAD_DIM),
                            jnp.tile(q_norm_b, width_b // HEAD_DIM) * scale,
                            jnp.tile(k_norm_b, kv_b // HEAD_DIM)])[None, :]
    n = 2 * LANES
    bd = (jnp.arange(n)[:, None] // HEAD_DIM == jnp.arange(n)[None, :] // HEAD_DIM).astype(BF16)
    return dict(width_a=width_a, width_b=width_b, kv_b=kv_b,
                norm_attn=norm_attn[None, :], w_in=w.astype(BF16), gcat=gcat, bd=bd,
                sink=sink_b * LOG2E, out_norm_a=out_norm_a[None, :], out_norm_b=out_norm_b[None, :],
                w_o=w_o.astype(BF16), norm_mlp=norm_mlp[None, :],
                w_up=w_up.astype(BF16), w_down=w_down.astype(BF16))


def kernel(x_prompt, x_sample, norm_attn, w_in, q_norm_a, k_norm_a, q_norm_b, k_norm_b, sink_b,
           out_norm_a, out_norm_b, w_o, norm_mlp, w_up, w_down):
    yp, ys = x_prompt, x_sample
    for l in range(w_in.shape[0]):
        p = _prepare(norm_attn[l], w_in[l], q_norm_a[l], k_norm_a[l], q_norm_b[l], k_norm_b[l],
                     sink_b[l], out_norm_a[l], out_norm_b[l], w_o[l], norm_mlp[l], w_up[l],
                     w_down[l])
        yp = _layer(yp, p)
        ys = _layer(ys, p)
    return (yp, ys)
```

```python
import functools

import jax
import jax.numpy as jnp
from jax import lax
from jax.experimental import pallas as pl
from jax.experimental.pallas import tpu as pltpu

HEAD_DIM = 64
PAIR = 2 * HEAD_DIM
DILATED_BRANCHES = ((128, 1), (512, 4), (2048, 16))
WINDOW_B = 128
ROPE_THETA = 500000.0
ROPE_DIM = HEAD_DIM // 4
ROPE_HALF = ROPE_DIM // 2
EPS = 1e-6
NEG_BIG = -1e30
LOG2E = 1.4426950408889634
LANES = 128
V7X_VMEM_BYTES = 64 * 1024 * 1024
VMEM_LIMIT = V7X_VMEM_BYTES - 8 * 1024 * 1024

F32 = jnp.float32
BF16 = jnp.bfloat16


def _dot(a, b):
    return jnp.dot(a, b, preferred_element_type=F32)


def _dot_nt(a, b):
    return lax.dot_general(a, b, (((1,), (1,)), ((), ())), preferred_element_type=F32)


def _proj_kernel(x_ref, g_ref, w_ref, gcat_ref, bd_ref, c_ref, s1_ref, s2_ref,
                 qa1, ka1, va1, qa4, ka4, va4, qa16, ka16, va16, qb, kb, vb,
                 stage_ref, stage4_ref, *, width_a, width_b, kv_b):
    tm = x_ref.shape[0]
    x = x_ref[...]
    ms = jnp.mean(x * x, axis=-1, keepdims=True)
    xn = (x * lax.rsqrt(ms + EPS) * g_ref[...]).astype(BF16)
    cos = c_ref[...]
    sin_up = s1_ref[...]
    sin_dn = s2_ref[...]
    bd = bd_ref[...]

    def head_norm_rope(h, col0):
        width = h.shape[1]
        sq = (h * h).astype(BF16)
        blocks = []
        sub = min(width, bd.shape[0])
        for c in range(0, width, sub):
            msq = _dot(sq[:, c:c + sub], bd[:sub, :sub]) * (1.0 / HEAD_DIM)
            y = h[:, c:c + sub] * lax.rsqrt(msq + EPS) * gcat_ref[:, col0 + c:col0 + c + sub]
            for b in range(0, sub, LANES):
                yb = y[:, b:b + LANES]
                up = pltpu.roll(yb, LANES - ROPE_HALF, 1)
                dn = pltpu.roll(yb, ROPE_HALF, 1)
                blocks.append(yb * cos + up * sin_up + dn * sin_dn)
        return blocks

    def write_dilated(blocks, outs, slab0):
        o1, o4, o16 = outs
        n4, n16 = tm // 4, tm // 16
        for p, blk in enumerate(blocks):
            slab = slab0 + p
            o1[p, 0] = blk.astype(BF16)
            stage_ref[slab] = blk
            for r4 in range(4):
                cls = stage_ref[slab, pl.ds(r4, n4, stride=4), :]
                o4[p, r4] = cls.astype(BF16)
                stage4_ref[slab, r4 * n4:(r4 + 1) * n4, :] = cls
            for r4 in range(4):
                for c in range(4):
                    sub = stage4_ref[slab, pl.ds(r4 * n4 + c, n16, stride=4), :]
                    o16[p, 4 * c + r4] = sub.astype(BF16)

    def write_both_halves(blk, out):
        lo = lax.broadcasted_iota(jnp.int32, (tm, LANES), 1) < HEAD_DIM
        sw = pltpu.roll(blk, HEAD_DIM, 1)
        out[0] = jnp.where(lo, blk, sw).astype(BF16)
        out[1] = jnp.where(lo, sw, blk).astype(BF16)

    def lane_blocks(h):
        return [h[:, b:b + LANES] for b in range(0, h.shape[1], LANES)]

    def write_qb(blocks):
        for p, blk in enumerate(blocks):
            qb[p] = blk.astype(BF16)

    n_pa = width_a // LANES
    chunks = [
        (width_a, lambda h, c: write_dilated(head_norm_rope(h, c), (qa1, qa4, qa16), 0)),
        (width_a, lambda h, c: write_dilated(head_norm_rope(h, c), (ka1, ka4, ka16), n_pa)),
        (width_b, lambda h, c: write_qb(head_norm_rope(h, c))),
        (kv_b, lambda h, c: write_both_halves(head_norm_rope(h, c)[0], kb)),
        (width_a, lambda h, c: write_dilated(lane_blocks(h), (va1, va4, va16), 2 * n_pa)),
        (kv_b, lambda h, c: write_both_halves(h, vb)),
    ]
    starts = [sum(w for w, _ in chunks[:i]) for i in range(len(chunks))]

    def project(i):
        return _dot(xn, w_ref[:, starts[i]:starts[i] + chunks[i][0]])

    h_next = project(0)
    for i, (_, consume) in enumerate(chunks):
        h = h_next
        if i + 1 < len(chunks):
            h_next = project(i + 1)
        consume(h, starts[i])


def _proj_call(x, g, w, gcat, bd, cos, sin_up, sin_dn, *, width_a, width_b, kv_b, tm):
    B, S, D = x.shape
    n_pa = width_a // PAIR
    n_pb = width_b // PAIR
    n_kv = kv_b // HEAD_DIM
    cols = w.shape[1]

    def dil_shape(d):
        return jax.ShapeDtypeStruct((B, n_pa, d, S // d, PAIR), BF16)

    def dil_spec(d):
        return pl.BlockSpec((None, n_pa, d, tm // d, PAIR), lambda b, i: (b, 0, 0, i, 0))

    const = lambda b, i: (0, 0)
    out_shape = []
    out_specs = []
    for d in (1, 4, 16):
        out_shape += [dil_shape(d)] * 3
        out_specs += [dil_spec(d)] * 3
    out_shape += [jax.ShapeDtypeStruct((B, n_pb, S, PAIR), BF16),
                  jax.ShapeDtypeStruct((B, n_kv, S, PAIR), BF16),
                  jax.ShapeDtypeStruct((B, n_kv, S, PAIR), BF16)]
    out_specs += [pl.BlockSpec((None, n_pb, tm, PAIR), lambda b, i: (b, 0, i, 0)),
                  pl.BlockSpec((None, n_kv, tm, PAIR), lambda b, i: (b, 0, i, 0)),
                  pl.BlockSpec((None, n_kv, tm, PAIR), lambda b, i: (b, 0, i, 0))]
    tab_spec = pl.BlockSpec((tm, LANES), lambda b, i: (i, 0))
    return pl.pallas_call(
        functools.partial(_proj_kernel, width_a=width_a, width_b=width_b, kv_b=kv_b),
        grid=(B, S // tm),
        in_specs=[pl.BlockSpec((None, tm, D), lambda b, i: (b, i, 0)),
                  pl.BlockSpec((1, D), const),
                  pl.BlockSpec((D, cols), const, pipeline_mode=pl.Buffered(1)),
                  pl.BlockSpec((1, gcat.shape[1]), const),
                  pl.BlockSpec(bd.shape, const),
                  tab_spec, tab_spec, tab_spec],
        out_specs=out_specs,
        out_shape=out_shape,
        scratch_shapes=[pltpu.VMEM((3 * n_pa, tm, LANES), F32)] * 2,
        compiler_params=pltpu.CompilerParams(
            dimension_semantics=("parallel", "parallel"), vmem_limit_bytes=VMEM_LIMIT),
        name="proj_qkv",
    )(x, g, w, gcat, bd, cos, sin_up, sin_dn)


def _attend(blocks, sinks, finish):
    bq = blocks[0][0][0].shape[0]
    lo = lax.broadcasted_iota(jnp.int32, (bq, PAIR), 1) < HEAD_DIM
    zero = jnp.zeros((bq, PAIR), BF16)

    def scores(i):
        q_tiles, k, _, _ = blocks[i]
        rows = []
        for q in q_tiles:
            rows.append(jnp.where(lo, q, zero))
            rows.append(jnp.where(lo, zero, q))
        return _dot_nt(jnp.concatenate(rows, axis=0), k)

    s_next = scores(0)
    for i, (q_tiles, _, v, bias) in enumerate(blocks):
        s = s_next
        if i + 1 < len(blocks):
            s_next = scores(i + 1)
        ps, ms = [], []
        for h in range(2 * len(q_tiles)):
            sh = s[h * bq:(h + 1) * bq] + bias
            m = jnp.max(sh, axis=-1, keepdims=True)
            if sinks is not None:
                m = jnp.maximum(m, sinks[h])
            ps.append(jnp.exp2(sh - m).astype(BF16))
            ms.append(m)
        pv = _dot(jnp.concatenate(ps, axis=0),
                  jnp.concatenate([v, jnp.ones_like(v)], axis=1))
        res = []
        for t in range(len(q_tiles)):
            a, b = 2 * t, 2 * t + 1
            pa, pb = pv[a * bq:(a + 1) * bq], pv[b * bq:(b + 1) * bq]
            la, lb = pa[:, PAIR:], pb[:, PAIR:]
            if sinks is not None:
                la = la + jnp.exp2(sinks[a] - ms[a])
                lb = lb + jnp.exp2(sinks[b] - ms[b])
            res.append((jnp.where(lo, pa[:, :PAIR], pb[:, :PAIR]),
                        jnp.where(lo, ms[a], ms[b]), jnp.where(lo, la, lb)))
        finish(i, res)


def _band_mask(bq, tk, offset, half_window):
    r = lax.broadcasted_iota(jnp.int32, (bq, tk), 0)
    c = lax.broadcasted_iota(jnp.int32, (bq, tk), 1)
    return jnp.abs(c - r + offset) <= half_window


def _branch_cfg(S, window, d):
    hw = window // (2 * d)
    seg = S // d
    bq = min(2 * hw, seg)
    tk = min(bq + 2 * hw, seg)
    return hw, seg, bq, tk


def _fill_bias(bias_ref, bq, tk, hw):
    for n in range(bias_ref.shape[0]):
        bias_ref[n] = jnp.where(_band_mask(bq, tk, -n * hw, hw), 0.0, NEG_BIG)


def _attn_a_kernel(q1, k1, v1, q4, k4, v4, q16, k16, v16, o_ref, st4_ref, st16_ref, tmp_ref,
                   *bias_refs, group):
    S = o_ref.shape[0]
    S4 = S // 4
    refs = {1: (q1, k1, v1), 4: (q4, k4, v4), 16: (q16, k16, v16)}
    assert [d for _, d in DILATED_BRANCHES] == [1, 4, 16]
    cfg = {}
    for bi, (window, d) in enumerate(DILATED_BRANCHES):
        cfg[d] = _branch_cfg(S, window, d) + (bias_refs[bi],)
        hw, seg, bq, tk, bias_ref = cfg[d]
        _fill_bias(bias_ref, bq, tk, hw)
    bq = cfg[1][2]
    assert cfg[4][2] == bq and cfg[16][2] == bq
    n4 = bq // 4

    def block(d, i0):
        hw, seg, _, tk, bias_ref = cfg[d]
        q_ref, k_ref, v_ref = refs[d]
        r = i0 // seg
        m0 = i0 - r * seg
        back = m0 - jnp.clip(m0 - hw, 0, seg - tk)
        ks = pl.multiple_of(i0 - back, hw)
        return (([q_ref[pl.ds(i0, bq), :]], k_ref[pl.ds(ks, tk), :], v_ref[pl.ds(ks, tk), :],
                 bias_ref[back // hw]), (r, m0))

    def dilated_step(j, carry):
        blocks, dests = [], []
        for g in range(group):
            i0 = pl.multiple_of((j * group + g) * bq, bq)
            for d in (4, 16):
                blk, (r, m0) = block(d, i0)
                blocks.append(blk)
                if d == 4:
                    dests.append((st4_ref, pl.ds(r + 4 * m0, bq, stride=4)))
                else:
                    dests.append((st16_ref, pl.ds((r % 4) * S4 + 4 * m0 + r // 4, bq, stride=4)))

        def finish(i, res):
            ref, rows = dests[i]
            for a, val in enumerate(res[0]):
                ref[a, rows, :] = val

        _attend(blocks, None, finish)
        return carry

    lax.fori_loop(0, S // (bq * group), dilated_step, 0)

    def local_step(j, carry):
        blocks, starts = [], []
        for g in range(group):
            i0 = pl.multiple_of((j * group + g) * bq, bq)
            blocks.append(block(1, i0)[0])
            starts.append(i0)
            for a in range(3):
                for r4 in range(4):
                    piece = st16_ref[a, pl.ds(r4 * S4 + i0 // 4, n4), :]
                    tmp_ref[3 * g + a, pl.ds(r4, n4, stride=4), :] = piece

        def finish(g, res):
            ((acc1, m1, l1),) = res
            acc4, m4, l4 = [st4_ref[a, pl.ds(starts[g], bq), :] for a in range(3)]
            acc16, m16, l16 = [tmp_ref[3 * g + a] for a in range(3)]
            m = jnp.maximum(jnp.maximum(m1, m4), m16)
            a1, a4, a16 = jnp.exp2(m1 - m), jnp.exp2(m4 - m), jnp.exp2(m16 - m)
            acc = a1 * acc1 + a4 * acc4 + a16 * acc16
            l = a1 * l1 + a4 * l4 + a16 * l16
            o_ref[pl.ds(starts[g], bq), :] = acc / l

        _attend(blocks, None, finish)
        return carry

    lax.fori_loop(0, S // (bq * group), local_step, 0)


def _attn_a_call(qkv, B, S, width_a, *, group):
    n_p = width_a // PAIR
    flat = [t.reshape(B, n_p, S, PAIR) for t in qkv]
    spec = pl.BlockSpec((None, None, S, PAIR), lambda b, p: (b, p, 0, 0))
    bias_shapes = []
    for window, d in DILATED_BRANCHES:
        hw, seg, bq, tk = _branch_cfg(S, window, d)
        bias_shapes.append(pltpu.VMEM(((tk - bq) // hw + 1, bq, tk), F32))
    return pl.pallas_call(
        functools.partial(_attn_a_kernel, group=group),
        grid=(B, n_p),
        in_specs=[spec] * 9,
        out_specs=pl.BlockSpec((None, S, PAIR), lambda b, p: (b, 0, p)),
        out_shape=jax.ShapeDtypeStruct((B, S, width_a), F32),
        scratch_shapes=[pltpu.VMEM((3, S, PAIR), F32)] * 2
        + [pltpu.VMEM((3 * group, _branch_cfg(S, *DILATED_BRANCHES[0])[2], PAIR), F32)]
        + bias_shapes,
        compiler_params=pltpu.CompilerParams(
            dimension_semantics=("parallel", "parallel"), vmem_limit_bytes=VMEM_LIMIT),
        name="attn_dilated",
    )(*flat)


def _attn_b_kernel(sink_ref, q_ref, k_ref, v_ref, o_ref, bias_ref, *, group):
    n_t, S, _ = q_ref.shape
    kv_head = pl.program_id(1)
    bq = WINDOW_B
    tk = min(3 * bq, S)
    sinks = [sink_ref[kv_head * 2 * n_t + h] for h in range(2 * n_t)]
    _fill_bias(bias_ref, bq, tk, WINDOW_B)

    def step(j, carry):
        blocks, starts = [], []
        for g in range(group):
            i0 = pl.multiple_of((j * group + g) * bq, bq)
            ks = pl.multiple_of(jnp.clip(i0 - WINDOW_B, 0, S - tk), WINDOW_B)
            blocks.append(([q_ref[t, pl.ds(i0, bq), :] for t in range(n_t)],
                           k_ref[pl.ds(ks, tk), :], v_ref[pl.ds(ks, tk), :],
                           bias_ref[(i0 - ks) // WINDOW_B]))
            starts.append(i0)
        def finish(g, res):
            for t, (acc, m, l) in enumerate(res):
                o_ref[pl.ds(starts[g], bq), t * PAIR:(t + 1) * PAIR] = acc / l

        _attend(blocks, sinks, finish)
        return carry

    lax.fori_loop(0, S // (bq * group), step, 0)


def _attn_b_call(sink, qb, kb, vb, *, group):
    B, n_p, S, _ = qb.shape
    n_kv = kb.shape[1]
    n_t = n_p // n_kv
    kv_spec = pl.BlockSpec((None, None, S, PAIR), lambda b, g: (b, g, 0, 0))
    bq = WINDOW_B
    tk = min(3 * bq, S)
    return pl.pallas_call(
        functools.partial(_attn_b_kernel, group=group),
        grid=(B, n_kv),
        in_specs=[pl.BlockSpec(memory_space=pltpu.SMEM),
                  pl.BlockSpec((None, n_t, S, PAIR), lambda b, g: (b, g, 0, 0)),
                  kv_spec, kv_spec],
        out_specs=pl.BlockSpec((None, S, n_t * PAIR), lambda b, g: (b, 0, g)),
        out_shape=jax.ShapeDtypeStruct((B, S, n_p * PAIR), F32),
        scratch_shapes=[pltpu.VMEM(((tk - bq) // WINDOW_B + 1, bq, tk), F32)],
        compiler_params=pltpu.CompilerParams(
            dimension_semantics=("parallel", "parallel"), vmem_limit_bytes=VMEM_LIMIT),
        name="attn_window",
    )(sink, qb, kb, vb)


def _rms(x, g):
    ms = jnp.mean(x * x, axis=-1, keepdims=True)
    return x * lax.rsqrt(ms + EPS) * g


def _mlp_kernel(oa_ref, ob_ref, x_ref, ga_ref, gb_ref, wo_ref, gm_ref, wu_ref, wd_ref, y_ref,
                *, ff_chunk):
    wa = oa_ref.shape[1]
    na = _rms(oa_ref[...], ga_ref[...]).astype(BF16)
    nb = _rms(ob_ref[...], gb_ref[...]).astype(BF16)
    h = x_ref[...] + _dot(na, wo_ref[:wa, :]) + _dot(nb, wo_ref[wa:, :])
    hn = _rms(h, gm_ref[...]).astype(BF16)
    acc = None
    for c in range(0, wu_ref.shape[1], ff_chunk):
        u = jnp.maximum(_dot(hn, wu_ref[:, c:c + ff_chunk]), 0.0)
        d = _dot((u * u).astype(BF16), wd_ref[c:c + ff_chunk, :])
        acc = d if acc is None else acc + d
    y_ref[...] = h + acc


def _mlp_call(oa, ob, x, ga, gb, wo, gm, wu, wd, *, tm, ff_chunk):
    B, S, D = x.shape
    const = lambda b, i: (0, 0)

    def resident(a):
        return pl.BlockSpec(a.shape, const, pipeline_mode=pl.Buffered(1))

    def rows(a):
        return pl.BlockSpec((None, tm, a.shape[2]), lambda b, i: (b, i, 0))

    return pl.pallas_call(
        functools.partial(_mlp_kernel, ff_chunk=ff_chunk),
        grid=(B, S // tm),
        in_specs=[rows(oa), rows(ob), rows(x), resident(ga), resident(gb), resident(wo),
                  resident(gm), resident(wu), resident(wd)],
        out_specs=rows(x),
        out_shape=jax.ShapeDtypeStruct((B, S, D), F32),
        compiler_params=pltpu.CompilerParams(
            dimension_semantics=("parallel", "parallel"), vmem_limit_bytes=VMEM_LIMIT),
        name="out_mlp",
    )(oa, ob, x, ga, gb, wo, gm, wu, wd)


def _rope_tables(S):
    inv = ROPE_THETA ** (-jnp.arange(ROPE_HALF, dtype=F32) * 2.0 / ROPE_DIM)
    ang = jnp.arange(S, dtype=F32)[:, None] * inv[None, :]
    cos, sin = jnp.cos(ang), jnp.sin(ang)
    ones = jnp.ones((S, HEAD_DIM - ROPE_DIM), F32)
    zeros = jnp.zeros((S, HEAD_DIM - ROPE_DIM), F32)
    z8 = jnp.zeros((S, ROPE_HALF), F32)
    head_cos = jnp.concatenate([cos, cos, ones], axis=1)
    head_up = jnp.concatenate([-sin, z8, zeros], axis=1)
    head_dn = jnp.concatenate([z8, sin, zeros], axis=1)
    rep = LANES // HEAD_DIM
    return tuple(jnp.tile(t, (1, rep)) for t in (head_cos, head_up, head_dn))


TILES = dict(tm_proj=512, tm_mlp=512, ff_chunk=512, group_a=16, group_b=8)


def _layer(x, p):
    B, S, D = x.shape
    width_a, width_b, kv_b = p["width_a"], p["width_b"], p["kv_b"]
    cos, sin_up, sin_dn = _rope_tables(S)
    outs = _proj_call(x, p["norm_attn"], p["w_in"], p["gcat"], p["bd"], cos, sin_up, sin_dn,
                      width_a=width_a, width_b=width_b, kv_b=kv_b, tm=TILES["tm_proj"])
    qa1, ka1, va1, qa4, ka4, va4, qa16, ka16, va16, qb, kb, vb = outs
    oa = _attn_a_call((qa1, ka1, va1, qa4, ka4, va4, qa16, ka16, va16), B, S, width_a,
                      group=TILES["group_a"])
    ob = _attn_b_call(p["sink"], qb, kb, vb, group=TILES["group_b"])
    return _mlp_call(oa, ob, x, p["out_norm_a"], p["out_norm_b"], p["w_o"], p["norm_mlp"],
                     p["w_up"], p["w_down"], tm=TILES["tm_mlp"], ff_chunk=TILES["ff_chunk"])


def _prepare(norm_attn, w_in, q_norm_a, k_norm_a, q_norm_b, k_norm_b, sink_b, out_norm_a,
             out_norm_b, w_o, norm_mlp, w_up, w_down):
    width_a = out_norm_a.shape[0]
    width_b = out_norm_b.shape[0]
    kv_b = (w_in.shape[1] - 3 * width_a - width_b) // 2
    o_qa, o_ka, o_va = 0, width_a, 2 * width_a
    o_qb = 3 * width_a
    o_kb = o_qb + width_b
    o_vb = o_kb + kv_b
    w = jnp.concatenate([w_in[:, o_qa:o_qa + width_a], w_in[:, o_ka:o_ka + width_a],
                         w_in[:, o_qb:o_qb + width_b], w_in[:, o_kb:o_kb + kv_b],
                         w_in[:, o_va:o_va + width_a], w_in[:, o_vb:o_vb + kv_b]], axis=1)
    scale = HEAD_DIM ** -0.5 * LOG2E
    gcat = jnp.concatenate([jnp.tile(q_norm_a, width_a // HEAD_DIM) * scale,
                            jnp.tile(k_norm_a, width_a // HEAD_DIM),
                            jnp.tile(q_norm_b, width_b // HEAD_DIM) * scale,
                            jnp.tile(k_norm_b, kv_b // HEAD_DIM)])[None, :]
    n = 2 * LANES
    bd = (jnp.arange(n)[:, None] // HEAD_DIM == jnp.arange(n)[None, :] // HEAD_DIM).astype(BF16)
    return dict(width_a=width_a, width_b=width_b, kv_b=kv_b,
                norm_attn=norm_attn[None, :], w_in=w.astype(BF16), gcat=gcat, bd=bd,
                sink=sink_b * LOG2E, out_norm_a=out_norm_a[None, :], out_norm_b=out_norm_b[None, :],
                w_o=w_o.astype(BF16), norm_mlp=norm_mlp[None, :],
                w_up=w_up.astype(BF16), w_down=w_down.astype(BF16))


def kernel(x_prompt, x_sample, norm_attn, w_in, q_norm_a, k_norm_a, q_norm_b, k_norm_b, sink_b,
           out_norm_a, out_norm_b, w_o, norm_mlp, w_up, w_down):
    yp, ys = x_prompt, x_sample
    for l in range(w_in.shape[0]):
        p = _prepare(norm_attn[l], w_in[l], q_norm_a[l], k_norm_a[l], q_norm_b[l], k_norm_b[l],
                     sink_b[l], out_norm_a[l], out_norm_b[l], w_o[l], norm_mlp[l], w_up[l],
                     w_down[l])
        yp = _layer(yp, p)
        ys = _layer(ys, p)
    return (yp, ys)
```

```python
import functools

import jax
import jax.numpy as jnp
from jax import lax
from jax.experimental import pallas as pl
from jax.experimental.pallas import tpu as pltpu

HEAD_DIM = 64
PAIR = 2 * HEAD_DIM
DILATED_BRANCHES = ((128, 1), (512, 4), (2048, 16))
WINDOW_B = 128
ROPE_THETA = 500000.0
ROPE_DIM = HEAD_DIM // 4
ROPE_HALF = ROPE_DIM // 2
EPS = 1e-6
NEG_BIG = -1e30
LOG2E = 1.4426950408889634
LANES = 128
V7X_VMEM_BYTES = 64 * 1024 * 1024
VMEM_LIMIT = V7X_VMEM_BYTES - 8 * 1024 * 1024

F32 = jnp.float32
BF16 = jnp.bfloat16


def _dot(a, b):
    return jnp.dot(a, b, preferred_element_type=F32)


def _dot_nt(a, b):
    return lax.dot_general(a, b, (((1,), (1,)), ((), ())), preferred_element_type=F32)


def _proj_kernel(x_ref, g_ref, w_ref, gcat_ref, bd_ref, c_ref, s1_ref, s2_ref,
                 qa1, ka1, va1, qa4, ka4, va4, qa16, ka16, va16, qb, kb, vb,
                 stage_ref, stage4_ref, *, width_a, width_b, kv_b):
    tm = x_ref.shape[0]
    x = x_ref[...]
    ms = jnp.mean(x * x, axis=-1, keepdims=True)
    xn = (x * lax.rsqrt(ms + EPS) * g_ref[...]).astype(BF16)
    cos = c_ref[...]
    sin_up = s1_ref[...]
    sin_dn = s2_ref[...]
    bd = bd_ref[...]

    def head_norm_rope(h, col0):
        width = h.shape[1]
        sq = (h * h).astype(BF16)
        blocks = []
        sub = min(width, bd.shape[0])
        for c in range(0, width, sub):
            msq = _dot(sq[:, c:c + sub], bd[:sub, :sub])
            y = h[:, c:c + sub] * lax.rsqrt(msq + EPS) * gcat_ref[:, col0 + c:col0 + c + sub]
            for b in range(0, sub, LANES):
                yb = y[:, b:b + LANES]
                up = pltpu.roll(yb, LANES - ROPE_HALF, 1)
                dn = pltpu.roll(yb, ROPE_HALF, 1)
                blocks.append(yb * cos + up * sin_up + dn * sin_dn)
        return blocks

    def write_dilated(blocks, outs, slab0):
        o1, o4, o16 = outs
        n4, n16 = tm // 4, tm // 16
        for p, blk in enumerate(blocks):
            slab = slab0 + p
            o1[p, 0] = blk.astype(BF16)
            stage_ref[slab] = blk
            for r4 in range(4):
                cls = stage_ref[slab, pl.ds(r4, n4, stride=4), :]
                o4[p, r4] = cls.astype(BF16)
                stage4_ref[slab, r4 * n4:(r4 + 1) * n4, :] = cls
            for r4 in range(4):
                for c in range(4):
                    sub = stage4_ref[slab, pl.ds(r4 * n4 + c, n16, stride=4), :]
                    o16[p, 4 * c + r4] = sub.astype(BF16)

    def write_both_halves(blk, out):
        lo = lax.broadcasted_iota(jnp.int32, (tm, LANES), 1) < HEAD_DIM
        sw = pltpu.roll(blk, HEAD_DIM, 1)
        out[0] = jnp.where(lo, blk, sw).astype(BF16)
        out[1] = jnp.where(lo, sw, blk).astype(BF16)

    def lane_blocks(h):
        return [h[:, b:b + LANES] for b in range(0, h.shape[1], LANES)]

    def write_qb(blocks):
        for p, blk in enumerate(blocks):
            qb[p] = blk.astype(BF16)

    n_pa = width_a // LANES
    def write_kv_b(h, c):
        write_both_halves(head_norm_rope(h[:, :kv_b], c)[0], kb)
        write_both_halves(h[:, kv_b:], vb)

    chunks = [
        (width_a, lambda h, c: write_dilated(head_norm_rope(h, c), (qa1, qa4, qa16), 0)),
        (width_a, lambda h, c: write_dilated(head_norm_rope(h, c), (ka1, ka4, ka16), n_pa)),
        (width_b, lambda h, c: write_qb(head_norm_rope(h, c))),
        (width_a, lambda h, c: write_dilated(lane_blocks(h), (va1, va4, va16), 2 * n_pa)),
        (2 * kv_b, write_kv_b),
    ]
    starts = [sum(w for w, _ in chunks[:i]) for i in range(len(chunks))]

    def project(i):
        return _dot(xn, w_ref[:, starts[i]:starts[i] + chunks[i][0]])

    h_next = project(0)
    for i, (_, consume) in enumerate(chunks):
        h = h_next
        if i + 1 < len(chunks):
            h_next = project(i + 1)
        consume(h, starts[i])


def _proj_call(x, g, w, gcat, bd, cos, sin_up, sin_dn, *, width_a, width_b, kv_b, tm):
    B, S, D = x.shape
    n_pa = width_a // PAIR
    n_pb = width_b // PAIR
    n_kv = kv_b // HEAD_DIM
    cols = w.shape[1]

    def dil_shape(d):
        return jax.ShapeDtypeStruct((B, n_pa, d, S // d, PAIR), BF16)

    def dil_spec(d):
        return pl.BlockSpec((None, n_pa, d, tm // d, PAIR), lambda b, i: (b, 0, 0, i, 0))

    const = lambda b, i: (0, 0)
    out_shape = []
    out_specs = []
    for d in (1, 4, 16):
        out_shape += [dil_shape(d)] * 3
        out_specs += [dil_spec(d)] * 3
    out_shape += [jax.ShapeDtypeStruct((B, n_pb, S, PAIR), BF16),
                  jax.ShapeDtypeStruct((B, n_kv, S, PAIR), BF16),
                  jax.ShapeDtypeStruct((B, n_kv, S, PAIR), BF16)]
    out_specs += [pl.BlockSpec((None, n_pb, tm, PAIR), lambda b, i: (b, 0, i, 0)),
                  pl.BlockSpec((None, n_kv, tm, PAIR), lambda b, i: (b, 0, i, 0)),
                  pl.BlockSpec((None, n_kv, tm, PAIR), lambda b, i: (b, 0, i, 0))]
    tab_spec = pl.BlockSpec((tm, LANES), lambda b, i: (i, 0))
    return pl.pallas_call(
        functools.partial(_proj_kernel, width_a=width_a, width_b=width_b, kv_b=kv_b),
        grid=(B, S // tm),
        in_specs=[pl.BlockSpec((None, tm, D), lambda b, i: (b, i, 0)),
                  pl.BlockSpec((1, D), const),
                  pl.BlockSpec((D, cols), const, pipeline_mode=pl.Buffered(1)),
                  pl.BlockSpec((1, gcat.shape[1]), const),
                  pl.BlockSpec(bd.shape, const),
                  tab_spec, tab_spec, tab_spec],
        out_specs=out_specs,
        out_shape=out_shape,
        scratch_shapes=[pltpu.VMEM((3 * n_pa, tm, LANES), F32)] * 2,
        compiler_params=pltpu.CompilerParams(
            dimension_semantics=("parallel", "parallel"), vmem_limit_bytes=VMEM_LIMIT),
        name="proj_qkv",
    )(x, g, w, gcat, bd, cos, sin_up, sin_dn)


def _attend(blocks, sinks, finish):
    bq = blocks[0][0][0].shape[0]
    lo = lax.broadcasted_iota(jnp.int32, (bq, PAIR), 1) < HEAD_DIM
    zero = jnp.zeros((bq, PAIR), BF16)

    def scores(i):
        q_tiles, k, _, _ = blocks[i]
        rows = []
        for q in q_tiles:
            rows.append(jnp.where(lo, q, zero))
            rows.append(jnp.where(lo, zero, q))
        return _dot_nt(jnp.concatenate(rows, axis=0), k)

    s_next = scores(0)
    for i, (q_tiles, _, v, bias) in enumerate(blocks):
        s = s_next
        if i + 1 < len(blocks):
            s_next = scores(i + 1)
        ps, ms = [], []
        for h in range(2 * len(q_tiles)):
            sh = s[h * bq:(h + 1) * bq] + bias
            m = jnp.max(sh, axis=-1, keepdims=True)
            if sinks is not None:
                m = jnp.maximum(m, sinks[h])
            ps.append(jnp.exp2(sh - m).astype(BF16))
            ms.append(m)
        pv = _dot(jnp.concatenate(ps, axis=0),
                  jnp.concatenate([v, jnp.ones_like(v)], axis=1))
        res = []
        for t in range(len(q_tiles)):
            a, b = 2 * t, 2 * t + 1
            pa, pb = pv[a * bq:(a + 1) * bq], pv[b * bq:(b + 1) * bq]
            la, lb = pa[:, PAIR:], pb[:, PAIR:]
            if sinks is not None:
                la = la + jnp.exp2(sinks[a] - ms[a])
                lb = lb + jnp.exp2(sinks[b] - ms[b])
            res.append((jnp.where(lo, pa[:, :PAIR], pb[:, :PAIR]),
                        jnp.where(lo, ms[a], ms[b]), jnp.where(lo, la, lb)))
        finish(i, res)


def _band_mask(bq, tk, offset, half_window):
    r = lax.broadcasted_iota(jnp.int32, (bq, tk), 0)
    c = lax.broadcasted_iota(jnp.int32, (bq, tk), 1)
    return jnp.abs(c - r + offset) <= half_window


def _branch_cfg(S, window, d):
    hw = window // (2 * d)
    seg = S // d
    bq = min(2 * hw, seg)
    tk = min(bq + 2 * hw, seg)
    return hw, seg, bq, tk


def _fill_bias(bias_ref, bq, tk, hw):
    for n in range(bias_ref.shape[0]):
        bias_ref[n] = jnp.where(_band_mask(bq, tk, -n * hw, hw), 0.0, NEG_BIG)


def _attn_a_kernel(q1, k1, v1, q4, k4, v4, q16, k16, v16, o_ref, st4_ref, st16_ref, tmp_ref,
                   *bias_refs, group):
    S = o_ref.shape[0]
    S4 = S // 4
    refs = {1: (q1, k1, v1), 4: (q4, k4, v4), 16: (q16, k16, v16)}
    assert [d for _, d in DILATED_BRANCHES] == [1, 4, 16]
    cfg = {}
    for bi, (window, d) in enumerate(DILATED_BRANCHES):
        cfg[d] = _branch_cfg(S, window, d) + (bias_refs[bi],)
        hw, seg, bq, tk, bias_ref = cfg[d]
        _fill_bias(bias_ref, bq, tk, hw)
    bq = cfg[1][2]
    assert cfg[4][2] == bq and cfg[16][2] == bq
    n4 = bq // 4

    def block(d, i0):
        hw, seg, _, tk, bias_ref = cfg[d]
        q_ref, k_ref, v_ref = refs[d]
        r = i0 // seg
        m0 = i0 - r * seg
        back = m0 - jnp.clip(m0 - hw, 0, seg - tk)
        ks = pl.multiple_of(i0 - back, hw)
        return (([q_ref[pl.ds(i0, bq), :]], k_ref[pl.ds(ks, tk), :], v_ref[pl.ds(ks, tk), :],
                 bias_ref[back // hw]), (r, m0))

    def dilated_step(j, carry):
        blocks, dests = [], []
        for g in range(group):
            i0 = pl.multiple_of((j * group + g) * bq, bq)
            for d in (4, 16):
                blk, (r, m0) = block(d, i0)
                blocks.append(blk)
                if d == 4:
                    dests.append((st4_ref, pl.ds(r + 4 * m0, bq, stride=4)))
                else:
                    dests.append((st16_ref, pl.ds((r % 4) * S4 + 4 * m0 + r // 4, bq, stride=4)))

        def finish(i, res):
            ref, rows = dests[i]
            for a, val in enumerate(res[0]):
                ref[a, rows, :] = val

        _attend(blocks, None, finish)
        return carry

    lax.fori_loop(0, S // (bq * group), dilated_step, 0)

    def local_step(j, carry):
        blocks, starts = [], []
        for g in range(group):
            i0 = pl.multiple_of((j * group + g) * bq, bq)
            blocks.append(block(1, i0)[0])
            starts.append(i0)
            for a in range(3):
                for r4 in range(4):
                    piece = st16_ref[a, pl.ds(r4 * S4 + i0 // 4, n4), :]
                    tmp_ref[3 * g + a, pl.ds(r4, n4, stride=4), :] = piece

        def finish(g, res):
            ((acc1, m1, l1),) = res
            acc4, m4, l4 = [st4_ref[a, pl.ds(starts[g], bq), :] for a in range(3)]
            acc16, m16, l16 = [tmp_ref[3 * g + a] for a in range(3)]
            m = jnp.maximum(jnp.maximum(m1, m4), m16)
            a1, a4, a16 = jnp.exp2(m1 - m), jnp.exp2(m4 - m), jnp.exp2(m16 - m)
            acc = a1 * acc1 + a4 * acc4 + a16 * acc16
            l = a1 * l1 + a4 * l4 + a16 * l16
            o_ref[pl.ds(starts[g], bq), :] = acc / l

        _attend(blocks, None, finish)
        return carry

    lax.fori_loop(0, S // (bq * group), local_step, 0)


def _attn_a_call(qkv, B, S, width_a, *, group):
    n_p = width_a // PAIR
    flat = [t.reshape(B, n_p, S, PAIR) for t in qkv]
    spec = pl.BlockSpec((None, None, S, PAIR), lambda b, p: (b, p, 0, 0))
    bias_shapes = []
    for window, d in DILATED_BRANCHES:
        hw, seg, bq, tk = _branch_cfg(S, window, d)
        bias_shapes.append(pltpu.VMEM(((tk - bq) // hw + 1, bq, tk), F32))
    return pl.pallas_call(
        functools.partial(_attn_a_kernel, group=group),
        grid=(B, n_p),
        in_specs=[spec] * 9,
        out_specs=pl.BlockSpec((None, S, PAIR), lambda b, p: (b, 0, p)),
        out_shape=jax.ShapeDtypeStruct((B, S, width_a), F32),
        scratch_shapes=[pltpu.VMEM((3, S, PAIR), F32)] * 2
        + [pltpu.VMEM((3 * group, _branch_cfg(S, *DILATED_BRANCHES[0])[2], PAIR), F32)]
        + bias_shapes,
        compiler_params=pltpu.CompilerParams(
            dimension_semantics=("parallel", "parallel"), vmem_limit_bytes=VMEM_LIMIT),
        name="attn_dilated",
    )(*flat)


def _attn_b_kernel(sink_ref, q_ref, k_ref, v_ref, o_ref, bias_ref, *, group):
    n_t, S, _ = q_ref.shape
    kv_head = pl.program_id(1)
    bq = WINDOW_B
    tk = min(3 * bq, S)
    sinks = [sink_ref[kv_head * 2 * n_t + h] for h in range(2 * n_t)]
    _fill_bias(bias_ref, bq, tk, WINDOW_B)

    def step(j, carry):
        blocks, starts = [], []
        for g in range(group):
            i0 = pl.multiple_of((j * group + g) * bq, bq)
            ks = pl.multiple_of(jnp.clip(i0 - WINDOW_B, 0, S - tk), WINDOW_B)
            blocks.append(([q_ref[t, pl.ds(i0, bq), :] for t in range(n_t)],
                           k_ref[pl.ds(ks, tk), :], v_ref[pl.ds(ks, tk), :],
                           bias_ref[(i0 - ks) // WINDOW_B]))
            starts.append(i0)
        def finish(g, res):
            for t, (acc, m, l) in enumerate(res):
                o_ref[pl.ds(starts[g], bq), t * PAIR:(t + 1) * PAIR] = acc / l

        _attend(blocks, sinks, finish)
        return carry

    lax.fori_loop(0, S // (bq * group), step, 0)


def _attn_b_call(sink, qb, kb, vb, *, group):
    B, n_p, S, _ = qb.shape
    n_kv = kb.shape[1]
    n_t = n_p // n_kv
    kv_spec = pl.BlockSpec((None, None, S, PAIR), lambda b, g: (b, g, 0, 0))
    bq = WINDOW_B
    tk = min(3 * bq, S)
    return pl.pallas_call(
        functools.partial(_attn_b_kernel, group=group),
        grid=(B, n_kv),
        in_specs=[pl.BlockSpec(memory_space=pltpu.SMEM),
                  pl.BlockSpec((None, n_t, S, PAIR), lambda b, g: (b, g, 0, 0)),
                  kv_spec, kv_spec],
        out_specs=pl.BlockSpec((None, S, n_t * PAIR), lambda b, g: (b, 0, g)),
        out_shape=jax.ShapeDtypeStruct((B, S, n_p * PAIR), F32),
        scratch_shapes=[pltpu.VMEM(((tk - bq) // WINDOW_B + 1, bq, tk), F32)],
        compiler_params=pltpu.CompilerParams(
            dimension_semantics=("parallel", "parallel"), vmem_limit_bytes=VMEM_LIMIT),
        name="attn_window",
    )(sink, qb, kb, vb)


def _rms(x, g):
    ms = jnp.mean(x * x, axis=-1, keepdims=True)
    return x * lax.rsqrt(ms + EPS) * g


def _mlp_kernel(oa_ref, ob_ref, x_ref, ga_ref, gb_ref, wo_ref, gm_ref, wu_ref, wd_ref, y_ref,
                *, ff_chunk):
    wa = oa_ref.shape[1]
    na = _rms(oa_ref[...], ga_ref[...]).astype(BF16)
    nb = _rms(ob_ref[...], gb_ref[...]).astype(BF16)
    h = x_ref[...] + _dot(na, wo_ref[:wa, :]) + _dot(nb, wo_ref[wa:, :])
    hn = _rms(h, gm_ref[...]).astype(BF16)
    acc = None
    for c in range(0, wu_ref.shape[1], ff_chunk):
        u = jnp.maximum(_dot(hn, wu_ref[:, c:c + ff_chunk]), 0.0)
        d = _dot((u * u).astype(BF16), wd_ref[c:c + ff_chunk, :])
        acc = d if acc is None else acc + d
    y_ref[...] = h + acc


def _mlp_call(oa, ob, x, ga, gb, wo, gm, wu, wd, *, tm, ff_chunk):
    B, S, D = x.shape
    const = lambda b, i: (0, 0)

    def resident(a):
        return pl.BlockSpec(a.shape, const, pipeline_mode=pl.Buffered(1))

    def rows(a):
        return pl.BlockSpec((None, tm, a.shape[2]), lambda b, i: (b, i, 0))

    return pl.pallas_call(
        functools.partial(_mlp_kernel, ff_chunk=ff_chunk),
        grid=(B, S // tm),
        in_specs=[rows(oa), rows(ob), rows(x), resident(ga), resident(gb), resident(wo),
                  resident(gm), resident(wu), resident(wd)],
        out_specs=rows(x),
        out_shape=jax.ShapeDtypeStruct((B, S, D), F32),
        compiler_params=pltpu.CompilerParams(
            dimension_semantics=("parallel", "parallel"), vmem_limit_bytes=VMEM_LIMIT),
        name="out_mlp",
    )(oa, ob, x, ga, gb, wo, gm, wu, wd)


def _rope_tables(S):
    inv = ROPE_THETA ** (-jnp.arange(ROPE_HALF, dtype=F32) * 2.0 / ROPE_DIM)
    ang = jnp.arange(S, dtype=F32)[:, None] * inv[None, :]
    cos, sin = jnp.cos(ang), jnp.sin(ang)
    ones = jnp.ones((S, HEAD_DIM - ROPE_DIM), F32)
    zeros = jnp.zeros((S, HEAD_DIM - ROPE_DIM), F32)
    z8 = jnp.zeros((S, ROPE_HALF), F32)
    head_cos = jnp.concatenate([cos, cos, ones], axis=1)
    head_up = jnp.concatenate([-sin, z8, zeros], axis=1)
    head_dn = jnp.concatenate([z8, sin, zeros], axis=1)
    rep = LANES // HEAD_DIM
    return tuple(jnp.tile(t, (1, rep)) for t in (head_cos, head_up, head_dn))


TILES = dict(tm_proj=512, tm_mlp=1024, ff_chunk=512, group_a=16, group_b=8)


def _layer(x, p):
    B, S, D = x.shape
    width_a, width_b, kv_b = p["width_a"], p["width_b"], p["kv_b"]
    cos, sin_up, sin_dn = _rope_tables(S)
    outs = _proj_call(x, p["norm_attn"], p["w_in"], p["gcat"], p["bd"], cos, sin_up, sin_dn,
                      width_a=width_a, width_b=width_b, kv_b=kv_b, tm=TILES["tm_proj"])
    qa1, ka1, va1, qa4, ka4, va4, qa16, ka16, va16, qb, kb, vb = outs
    oa = _attn_a_call((qa1, ka1, va1, qa4, ka4, va4, qa16, ka16, va16), B, S, width_a,
                      group=TILES["group_a"])
    ob = _attn_b_call(p["sink"], qb, kb, vb, group=TILES["group_b"])
    return _mlp_call(oa, ob, x, p["out_norm_a"], p["out_norm_b"], p["w_o"], p["norm_mlp"],
                     p["w_up"], p["w_down"], tm=TILES["tm_mlp"], ff_chunk=TILES["ff_chunk"])


def _prepare(norm_attn, w_in, q_norm_a, k_norm_a, q_norm_b, k_norm_b, sink_b, out_norm_a,
             out_norm_b, w_o, norm_mlp, w_up, w_down):
    width_a = out_norm_a.shape[0]
    width_b = out_norm_b.shape[0]
    kv_b = (w_in.shape[1] - 3 * width_a - width_b) // 2
    o_qa, o_ka, o_va = 0, width_a, 2 * width_a
    o_qb = 3 * width_a
    o_kb = o_qb + width_b
    o_vb = o_kb + kv_b
    w = jnp.concatenate([w_in[:, o_qa:o_qa + width_a], w_in[:, o_ka:o_ka + width_a],
                         w_in[:, o_qb:o_qb + width_b], w_in[:, o_va:o_va + width_a],
                         w_in[:, o_kb:o_kb + kv_b], w_in[:, o_vb:o_vb + kv_b]], axis=1)
    scale = HEAD_DIM ** -0.5 * LOG2E
    gcat = jnp.concatenate([jnp.tile(q_norm_a, width_a // HEAD_DIM) * scale,
                            jnp.tile(k_norm_a, width_a // HEAD_DIM),
                            jnp.tile(q_norm_b, width_b // HEAD_DIM) * scale,
                            jnp.ones((width_a,), F32),
                            jnp.tile(k_norm_b, kv_b // HEAD_DIM)])[None, :]
    n = 2 * LANES
    same_head = jnp.arange(n)[:, None] // HEAD_DIM == jnp.arange(n)[None, :] // HEAD_DIM
    bd = (same_head.astype(F32) / HEAD_DIM).astype(BF16)
    return dict(width_a=width_a, width_b=width_b, kv_b=kv_b,
                norm_attn=norm_attn[None, :], w_in=w.astype(BF16), gcat=gcat, bd=bd,
                sink=sink_b * LOG2E, out_norm_a=out_norm_a[None, :], out_norm_b=out_norm_b[None, :],
                w_o=w_o.astype(BF16), norm_mlp=norm_mlp[None, :],
                w_up=w_up.astype(BF16), w_down=w_down.astype(BF16))


def kernel(x_prompt, x_sample, norm_attn, w_in, q_norm_a, k_norm_a, q_norm_b, k_norm_b, sink_b,
           out_norm_a, out_norm_b, w_o, norm_mlp, w_up, w_down):
    yp, ys = x_prompt, x_sample
    for l in range(w_in.shape[0]):
        p = _prepare(norm_attn[l], w_in[l], q_norm_a[l], k_norm_a[l], q_norm_b[l], k_norm_b[l],
                     sink_b[l], out_norm_a[l], out_norm_b[l], w_o[l], norm_mlp[l], w_up[l],
                     w_down[l])
        yp = _layer(yp, p)
        ys = _layer(ys, p)
    return (yp, ys)
```

```python
import functools

import jax
import jax.numpy as jnp
from jax import lax
from jax.experimental import pallas as pl
from jax.experimental.pallas import tpu as pltpu

HEAD_DIM = 64
PAIR = 2 * HEAD_DIM
DILATED_BRANCHES = ((128, 1), (512, 4), (2048, 16))
WINDOW_B = 128
ROPE_THETA = 500000.0
ROPE_DIM = HEAD_DIM // 4
ROPE_HALF = ROPE_DIM // 2
EPS = 1e-6
NEG_BIG = -1e30
LOG2E = 1.4426950408889634
LANES = 128
V7X_VMEM_BYTES = 64 * 1024 * 1024

F32 = jnp.float32
BF16 = jnp.bfloat16


def _nbytes(shape, dtype):
    n = jnp.dtype(dtype).itemsize
    for s in shape:
        n *= s
    return n


def _vmem_limit(pipelined, resident, scratch, values):
    return min(2 * pipelined + resident + scratch + values, V7X_VMEM_BYTES)


def _dot(a, b):
    return jnp.dot(a, b, preferred_element_type=F32)


def _dot_nt(a, b):
    return lax.dot_general(a, b, (((1,), (1,)), ((), ())), preferred_element_type=F32)


def _proj_kernel(x_ref, g_ref, w_ref, gcat_ref, bd_ref, cos_ref, sin_ref,
                 qa1, ka1, va1, qa4, ka4, va4, qa16, ka16, va16, qb, kb, vb,
                 stage_ref, stage4_ref, *, width_a, width_b, kv_b):
    tm = x_ref.shape[0]
    x = x_ref[...]
    ms = jnp.mean(x * x, axis=-1, keepdims=True)
    xn = (x * lax.rsqrt(ms + EPS) * g_ref[...]).astype(BF16)
    cos = cos_ref[...]
    sin = sin_ref[...]
    first_half = lax.broadcasted_iota(jnp.int32, (tm, LANES), 1) % HEAD_DIM < ROPE_HALF
    bd = bd_ref[...]

    def head_norm_rope(h, col0):
        width = h.shape[1]
        sq = (h * h).astype(BF16)
        blocks = []
        sub = min(width, bd.shape[0])
        for c in range(0, width, sub):
            msq = _dot(sq[:, c:c + sub], bd[:sub, :sub])
            y = h[:, c:c + sub] * lax.rsqrt(msq + EPS) * gcat_ref[:, col0 + c:col0 + c + sub]
            for b in range(0, sub, LANES):
                yb = y[:, b:b + LANES]
                up = pltpu.roll(yb, LANES - ROPE_HALF, 1)
                dn = pltpu.roll(yb, ROPE_HALF, 1)
                blocks.append(yb * cos + jnp.where(first_half, up, dn) * sin)
        return blocks

    def write_dilated(blocks, outs, slab0):
        o1, o4, o16 = outs
        n4, n16 = tm // 4, tm // 16
        for p, blk in enumerate(blocks):
            slab = slab0 + p
            o1[p, 0] = blk.astype(BF16)
            stage_ref[slab] = blk
            for r4 in range(4):
                cls = stage_ref[slab, pl.ds(r4, n4, stride=4), :]
                o4[p, r4] = cls.astype(BF16)
                stage4_ref[slab, r4 * n4:(r4 + 1) * n4, :] = cls
            for r4 in range(4):
                for c in range(4):
                    sub = stage4_ref[slab, pl.ds(r4 * n4 + c, n16, stride=4), :]
                    o16[p, 4 * c + r4] = sub.astype(BF16)

    def write_both_halves(blk, out):
        lo = lax.broadcasted_iota(jnp.int32, (tm, LANES), 1) < HEAD_DIM
        sw = pltpu.roll(blk, HEAD_DIM, 1)
        out[0] = jnp.where(lo, blk, sw).astype(BF16)
        out[1] = jnp.where(lo, sw, blk).astype(BF16)

    def lane_blocks(h):
        return [h[:, b:b + LANES] for b in range(0, h.shape[1], LANES)]

    def write_qb(blocks):
        for p, blk in enumerate(blocks):
            qb[p] = blk.astype(BF16)

    def write_kv_b(h, c):
        write_both_halves(head_norm_rope(h[:, :kv_b], c)[0], kb)
        write_both_halves(h[:, kv_b:], vb)

    n_pa = width_a // LANES
    chunks = [
        (width_a, lambda h, c: write_dilated(head_norm_rope(h, c), (qa1, qa4, qa16), 0)),
        (width_a, lambda h, c: write_dilated(head_norm_rope(h, c), (ka1, ka4, ka16), n_pa)),
        (width_b, lambda h, c: write_qb(head_norm_rope(h, c))),
        (width_a, lambda h, c: write_dilated(lane_blocks(h), (va1, va4, va16), 2 * n_pa)),
        (2 * kv_b, write_kv_b),
    ]
    starts = [sum(w for w, _ in chunks[:i]) for i in range(len(chunks))]

    def project(i):
        return _dot(xn, w_ref[:, starts[i]:starts[i] + chunks[i][0]])

    h_next = project(0)
    for i, (_, consume) in enumerate(chunks):
        h = h_next
        if i + 1 < len(chunks):
            h_next = project(i + 1)
        consume(h, starts[i])


def _proj_call(x, g, w, gcat, bd, cos, sin, *, width_a, width_b, kv_b, tm):
    B, S, D = x.shape
    n_pa = width_a // PAIR
    n_pb = width_b // PAIR
    n_kv = kv_b // HEAD_DIM
    cols = w.shape[1]

    def dil_shape(d):
        return jax.ShapeDtypeStruct((B, n_pa, d, S // d, PAIR), BF16)

    def dil_spec(d):
        return pl.BlockSpec((None, n_pa, d, tm // d, PAIR), lambda b, i: (b, 0, 0, i, 0))

    const = lambda b, i: (0, 0)
    out_shape = []
    out_specs = []
    for d in (1, 4, 16):
        out_shape += [dil_shape(d)] * 3
        out_specs += [dil_spec(d)] * 3
    out_shape += [jax.ShapeDtypeStruct((B, n_pb, S, PAIR), BF16),
                  jax.ShapeDtypeStruct((B, n_kv, S, PAIR), BF16),
                  jax.ShapeDtypeStruct((B, n_kv, S, PAIR), BF16)]
    out_specs += [pl.BlockSpec((None, n_pb, tm, PAIR), lambda b, i: (b, 0, i, 0)),
                  pl.BlockSpec((None, n_kv, tm, PAIR), lambda b, i: (b, 0, i, 0)),
                  pl.BlockSpec((None, n_kv, tm, PAIR), lambda b, i: (b, 0, i, 0))]
    tab_spec = pl.BlockSpec((tm, LANES), lambda b, i: (i, 0))
    out_cols = 9 * width_a + width_b + 2 * n_kv * PAIR
    vmem = _vmem_limit(
        pipelined=_nbytes((tm, D), F32) + 2 * _nbytes((tm, LANES), F32)
        + _nbytes((tm, out_cols), BF16),
        resident=_nbytes(w.shape, BF16),
        scratch=2 * _nbytes((3 * n_pa, tm, LANES), F32),
        values=_nbytes((tm, D), BF16) + 6 * _nbytes((tm, width_a), F32))
    return pl.pallas_call(
        functools.partial(_proj_kernel, width_a=width_a, width_b=width_b, kv_b=kv_b),
        grid=(B, S // tm),
        in_specs=[pl.BlockSpec((None, tm, D), lambda b, i: (b, i, 0)),
                  pl.BlockSpec((1, D), const),
                  pl.BlockSpec((D, cols), const, pipeline_mode=pl.Buffered(1)),
                  pl.BlockSpec((1, gcat.shape[1]), const),
                  pl.BlockSpec(bd.shape, const),
                  tab_spec, tab_spec],
        out_specs=out_specs,
        out_shape=out_shape,
        scratch_shapes=[pltpu.VMEM((3 * n_pa, tm, LANES), F32)] * 2,
        compiler_params=pltpu.CompilerParams(
            dimension_semantics=("parallel", "parallel"), vmem_limit_bytes=vmem),
        name="proj_qkv",
    )(x, g, w, gcat, bd, cos, sin)


def _attend(blocks, sinks, finish):
    bq = blocks[0][0][0].shape[0]
    lo = lax.broadcasted_iota(jnp.int32, (bq, PAIR), 1) < HEAD_DIM
    zero = jnp.zeros((bq, PAIR), BF16)

    def scores(i):
        q_tiles, k, _, _ = blocks[i]
        rows = []
        for q in q_tiles:
            rows.append(jnp.where(lo, q, zero))
            rows.append(jnp.where(lo, zero, q))
        return _dot_nt(jnp.concatenate(rows, axis=0), k)

    s_next = scores(0)
    for i, (q_tiles, _, v, bias) in enumerate(blocks):
        s = s_next
        if i + 1 < len(blocks):
            s_next = scores(i + 1)
        ps, ms = [], []
        for h in range(2 * len(q_tiles)):
            sh = s[h * bq:(h + 1) * bq] + bias
            m = jnp.max(sh, axis=-1, keepdims=True)
            if sinks is not None:
                m = jnp.maximum(m, sinks[h])
            ps.append(jnp.exp2(sh - m).astype(BF16))
            ms.append(m)
        pv = _dot(jnp.concatenate(ps, axis=0),
                  jnp.concatenate([v, jnp.ones_like(v)], axis=1))
        res = []
        for t in range(len(q_tiles)):
            a, b = 2 * t, 2 * t + 1
            pa, pb = pv[a * bq:(a + 1) * bq], pv[b * bq:(b + 1) * bq]
            la, lb = pa[:, PAIR:], pb[:, PAIR:]
            if sinks is not None:
                la = la + jnp.exp2(sinks[a] - ms[a])
                lb = lb + jnp.exp2(sinks[b] - ms[b])
            res.append((jnp.where(lo, pa[:, :PAIR], pb[:, :PAIR]),
                        jnp.where(lo, ms[a], ms[b]), jnp.where(lo, la, lb)))
        finish(i, res)


def _band_mask(bq, tk, offset, half_window):
    r = lax.broadcasted_iota(jnp.int32, (bq, tk), 0)
    c = lax.broadcasted_iota(jnp.int32, (bq, tk), 1)
    return jnp.abs(c - r + offset) <= half_window


def _branch_cfg(S, window, d):
    hw = window // (2 * d)
    seg = S // d
    bq = min(2 * hw, seg)
    tk = min(bq + 2 * hw, seg)
    return hw, seg, bq, tk


def _fill_bias(bias_ref, bq, tk, hw):
    for n in range(bias_ref.shape[0]):
        bias_ref[n] = jnp.where(_band_mask(bq, tk, -n * hw, hw), 0.0, NEG_BIG)


def _attn_a_kernel(q1, k1, v1, q4, k4, v4, q16, k16, v16, o_ref, st4_ref, st16_ref, tmp_ref,
                   *bias_refs, group):
    S = o_ref.shape[0]
    S4 = S // 4
    refs = {1: (q1, k1, v1), 4: (q4, k4, v4), 16: (q16, k16, v16)}
    assert [d for _, d in DILATED_BRANCHES] == [1, 4, 16]
    cfg = {}
    for bi, (window, d) in enumerate(DILATED_BRANCHES):
        cfg[d] = _branch_cfg(S, window, d) + (bias_refs[bi],)
        hw, seg, bq, tk, bias_ref = cfg[d]
        _fill_bias(bias_ref, bq, tk, hw)
    bq = cfg[1][2]
    assert cfg[4][2] == bq and cfg[16][2] == bq
    n4 = bq // 4

    def block(d, i0):
        hw, seg, _, tk, bias_ref = cfg[d]
        q_ref, k_ref, v_ref = refs[d]
        r = i0 // seg
        m0 = i0 - r * seg
        back = m0 - jnp.clip(m0 - hw, 0, seg - tk)
        ks = pl.multiple_of(i0 - back, hw)
        return (([q_ref[pl.ds(i0, bq), :]], k_ref[pl.ds(ks, tk), :], v_ref[pl.ds(ks, tk), :],
                 bias_ref[back // hw]), (r, m0))

    def dilated_step(j, carry):
        blocks, dests = [], []
        for g in range(group):
            i0 = pl.multiple_of((j * group + g) * bq, bq)
            for d in (4, 16):
                blk, (r, m0) = block(d, i0)
                blocks.append(blk)
                if d == 4:
                    dests.append((st4_ref, pl.ds(r + 4 * m0, bq, stride=4)))
                else:
                    dests.append((st16_ref, pl.ds((r % 4) * S4 + 4 * m0 + r // 4, bq, stride=4)))

        def finish(i, res):
            ref, rows = dests[i]
            for a, val in enumerate(res[0]):
                ref[a, rows, :] = val

        _attend(blocks, None, finish)
        return carry

    lax.fori_loop(0, S // (bq * group), dilated_step, 0)

    def local_step(j, carry):
        blocks, starts = [], []
        for g in range(group):
            i0 = pl.multiple_of((j * group + g) * bq, bq)
            blocks.append(block(1, i0)[0])
            starts.append(i0)
            for a in range(3):
                for r4 in range(4):
                    piece = st16_ref[a, pl.ds(r4 * S4 + i0 // 4, n4), :]
                    tmp_ref[3 * g + a, pl.ds(r4, n4, stride=4), :] = piece

        def finish(g, res):
            ((acc1, m1, l1),) = res
            acc4, m4, l4 = [st4_ref[a, pl.ds(starts[g], bq), :] for a in range(3)]
            acc16, m16, l16 = [tmp_ref[3 * g + a] for a in range(3)]
            m = jnp.maximum(jnp.maximum(m1, m4), m16)
            a1, a4, a16 = jnp.exp2(m1 - m), jnp.exp2(m4 - m), jnp.exp2(m16 - m)
            acc = a1 * acc1 + a4 * acc4 + a16 * acc16
            l = a1 * l1 + a4 * l4 + a16 * l16
            o_ref[pl.ds(starts[g], bq), :] = acc / l

        _attend(blocks, None, finish)
        return carry

    lax.fori_loop(0, S // (bq * group), local_step, 0)


def _attn_a_call(qkv, B, S, width_a, *, group):
    n_p = width_a // PAIR
    flat = [t.reshape(B, n_p, S, PAIR) for t in qkv]
    spec = pl.BlockSpec((None, None, S, PAIR), lambda b, p: (b, p, 0, 0))
    scratch = [(3, S, PAIR)] * 2
    bq = _branch_cfg(S, *DILATED_BRANCHES[0])[2]
    scratch.append((3 * group, bq, PAIR))
    for window, d in DILATED_BRANCHES:
        hw, seg, bq, tk = _branch_cfg(S, window, d)
        scratch.append(((tk - bq) // hw + 1, bq, tk))
    vmem = _vmem_limit(
        pipelined=9 * _nbytes((S, PAIR), BF16) + _nbytes((S, PAIR), F32),
        resident=0,
        scratch=sum(_nbytes(s, F32) for s in scratch),
        values=2 * group * (_nbytes((2 * bq, 2 * bq), F32) + _nbytes((2 * bq, 2 * bq), BF16)
                            + _nbytes((2 * bq, 2 * PAIR), F32)))
    return pl.pallas_call(
        functools.partial(_attn_a_kernel, group=group),
        grid=(B, n_p),
        in_specs=[spec] * 9,
        out_specs=pl.BlockSpec((None, S, PAIR), lambda b, p: (b, 0, p)),
        out_shape=jax.ShapeDtypeStruct((B, S, width_a), F32),
        scratch_shapes=[pltpu.VMEM(s, F32) for s in scratch],
        compiler_params=pltpu.CompilerParams(
            dimension_semantics=("parallel", "parallel"), vmem_limit_bytes=vmem),
        name="attn_dilated",
    )(*flat)


def _attn_b_kernel(sink_ref, q_ref, k_ref, v_ref, o_ref, bias_ref, *, group):
    n_t, S, _ = q_ref.shape
    kv_head = pl.program_id(1)
    bq = WINDOW_B
    tk = min(3 * bq, S)
    sinks = [sink_ref[kv_head * 2 * n_t + h] for h in range(2 * n_t)]
    _fill_bias(bias_ref, bq, tk, WINDOW_B)

    def step(j, carry):
        blocks, starts = [], []
        for g in range(group):
            i0 = pl.multiple_of((j * group + g) * bq, bq)
            ks = pl.multiple_of(jnp.clip(i0 - WINDOW_B, 0, S - tk), WINDOW_B)
            blocks.append(([q_ref[t, pl.ds(i0, bq), :] for t in range(n_t)],
                           k_ref[pl.ds(ks, tk), :], v_ref[pl.ds(ks, tk), :],
                           bias_ref[(i0 - ks) // WINDOW_B]))
            starts.append(i0)
        def finish(g, res):
            for t, (acc, m, l) in enumerate(res):
                o_ref[pl.ds(starts[g], bq), t * PAIR:(t + 1) * PAIR] = acc / l

        _attend(blocks, sinks, finish)
        return carry

    lax.fori_loop(0, S // (bq * group), step, 0)


def _attn_b_call(sink, qb, kb, vb, *, group):
    B, n_p, S, _ = qb.shape
    n_kv = kb.shape[1]
    n_t = n_p // n_kv
    kv_spec = pl.BlockSpec((None, None, S, PAIR), lambda b, g: (b, g, 0, 0))
    bq = WINDOW_B
    tk = min(3 * bq, S)
    bias_shape = ((tk - bq) // WINDOW_B + 1, bq, tk)
    rows = 2 * n_t * bq
    vmem = _vmem_limit(
        pipelined=(n_t + 2) * _nbytes((S, PAIR), BF16) + _nbytes((S, n_t * PAIR), F32),
        resident=0,
        scratch=_nbytes(bias_shape, F32),
        values=group * (_nbytes((rows, tk), F32) + _nbytes((rows, tk), BF16)
                        + _nbytes((rows, 2 * PAIR), F32)))
    return pl.pallas_call(
        functools.partial(_attn_b_kernel, group=group),
        grid=(B, n_kv),
        in_specs=[pl.BlockSpec(memory_space=pltpu.SMEM),
                  pl.BlockSpec((None, n_t, S, PAIR), lambda b, g: (b, g, 0, 0)),
                  kv_spec, kv_spec],
        out_specs=pl.BlockSpec((None, S, n_t * PAIR), lambda b, g: (b, 0, g)),
        out_shape=jax.ShapeDtypeStruct((B, S, n_p * PAIR), F32),
        scratch_shapes=[pltpu.VMEM(bias_shape, F32)],
        compiler_params=pltpu.CompilerParams(
            dimension_semantics=("parallel", "parallel"), vmem_limit_bytes=vmem),
        name="attn_window",
    )(sink, qb, kb, vb)


def _rms(x, g):
    ms = jnp.mean(x * x, axis=-1, keepdims=True)
    return x * lax.rsqrt(ms + EPS) * g


def _mlp_kernel(oa_ref, ob_ref, x_ref, ga_ref, gb_ref, wo_ref, gm_ref, wu_ref, wd_ref, y_ref,
                *, ff_chunk):
    wa = oa_ref.shape[1]
    na = _rms(oa_ref[...], ga_ref[...]).astype(BF16)
    nb = _rms(ob_ref[...], gb_ref[...]).astype(BF16)
    h = x_ref[...] + _dot(na, wo_ref[:wa, :]) + _dot(nb, wo_ref[wa:, :])
    hn = _rms(h, gm_ref[...]).astype(BF16)
    acc = None
    for c in range(0, wu_ref.shape[1], ff_chunk):
        u = jnp.maximum(_dot(hn, wu_ref[:, c:c + ff_chunk]), 0.0)
        d = _dot((u * u).astype(BF16), wd_ref[c:c + ff_chunk, :])
        acc = d if acc is None else acc + d
    y_ref[...] = h + acc


def _mlp_call(oa, ob, x, ga, gb, wo, gm, wu, wd, *, tm, ff_chunk):
    B, S, D = x.shape
    const = lambda b, i: (0, 0)

    def resident(a):
        return pl.BlockSpec(a.shape, const, pipeline_mode=pl.Buffered(1))

    def rows(a):
        return pl.BlockSpec((None, tm, a.shape[2]), lambda b, i: (b, i, 0))

    vmem = _vmem_limit(
        pipelined=_nbytes((tm, oa.shape[2] + ob.shape[2] + 2 * D), F32),
        resident=sum(_nbytes(a.shape, a.dtype) for a in (wo, wu, wd)),
        scratch=0,
        values=2 * _nbytes((tm, D), F32) + _nbytes((tm, D), BF16)
        + _nbytes((tm, ff_chunk), F32) + _nbytes((tm, ff_chunk), BF16))
    return pl.pallas_call(
        functools.partial(_mlp_kernel, ff_chunk=ff_chunk),
        grid=(B, S // tm),
        in_specs=[rows(oa), rows(ob), rows(x), resident(ga), resident(gb), resident(wo),
                  resident(gm), resident(wu), resident(wd)],
        out_specs=rows(x),
        out_shape=jax.ShapeDtypeStruct((B, S, D), F32),
        compiler_params=pltpu.CompilerParams(
            dimension_semantics=("parallel", "parallel"), vmem_limit_bytes=vmem),
        name="out_mlp",
    )(oa, ob, x, ga, gb, wo, gm, wu, wd)


def _rope_tables(S):
    inv = ROPE_THETA ** (-jnp.arange(ROPE_HALF, dtype=F32) * 2.0 / ROPE_DIM)
    ang = jnp.arange(S, dtype=F32)[:, None] * inv[None, :]
    cos, sin = jnp.cos(ang), jnp.sin(ang)
    ones = jnp.ones((S, HEAD_DIM - ROPE_DIM), F32)
    zeros = jnp.zeros((S, HEAD_DIM - ROPE_DIM), F32)
    head_cos = jnp.concatenate([cos, cos, ones], axis=1)
    head_sin = jnp.concatenate([-sin, sin, zeros], axis=1)
    rep = LANES // HEAD_DIM
    return tuple(jnp.tile(t, (1, rep)) for t in (head_cos, head_sin))


TILES = dict(tm_proj=512, tm_mlp=1024, ff_chunk=512, group_a=16, group_b=8)


def _layer(x, p):
    B, S, D = x.shape
    width_a, width_b, kv_b = p["width_a"], p["width_b"], p["kv_b"]
    cos, sin = _rope_tables(S)
    outs = _proj_call(x, p["norm_attn"], p["w_in"], p["gcat"], p["bd"], cos, sin,
                      width_a=width_a, width_b=width_b, kv_b=kv_b, tm=TILES["tm_proj"])
    qa1, ka1, va1, qa4, ka4, va4, qa16, ka16, va16, qb, kb, vb = outs
    oa = _attn_a_call((qa1, ka1, va1, qa4, ka4, va4, qa16, ka16, va16), B, S, width_a,
                      group=TILES["group_a"])
    ob = _attn_b_call(p["sink"], qb, kb, vb, group=TILES["group_b"])
    return _mlp_call(oa, ob, x, p["out_norm_a"], p["out_norm_b"], p["w_o"], p["norm_mlp"],
                     p["w_up"], p["w_down"], tm=TILES["tm_mlp"], ff_chunk=TILES["ff_chunk"])


def _prepare(norm_attn, w_in, q_norm_a, k_norm_a, q_norm_b, k_norm_b, sink_b, out_norm_a,
             out_norm_b, w_o, norm_mlp, w_up, w_down):
    width_a = out_norm_a.shape[0]
    width_b = out_norm_b.shape[0]
    kv_b = (w_in.shape[1] - 3 * width_a - width_b) // 2
    o_qa, o_ka, o_va = 0, width_a, 2 * width_a
    o_qb = 3 * width_a
    o_kb = o_qb + width_b
    o_vb = o_kb + kv_b
    w = jnp.concatenate([w_in[:, o_qa:o_qa + width_a], w_in[:, o_ka:o_ka + width_a],
                         w_in[:, o_qb:o_qb + width_b], w_in[:, o_va:o_va + width_a],
                         w_in[:, o_kb:o_kb + kv_b], w_in[:, o_vb:o_vb + kv_b]], axis=1)
    scale = HEAD_DIM ** -0.5 * LOG2E
    gcat = jnp.concatenate([jnp.tile(q_norm_a, width_a // HEAD_DIM) * scale,
                            jnp.tile(k_norm_a, width_a // HEAD_DIM),
                            jnp.tile(q_norm_b, width_b // HEAD_DIM) * scale,
                            jnp.ones((width_a,), F32),
                            jnp.tile(k_norm_b, kv_b // HEAD_DIM)])[None, :]
    n = 2 * LANES
    same_head = jnp.arange(n)[:, None] // HEAD_DIM == jnp.arange(n)[None, :] // HEAD_DIM
    bd = (same_head.astype(F32) / HEAD_DIM).astype(BF16)
    return dict(width_a=width_a, width_b=width_b, kv_b=kv_b,
                norm_attn=norm_attn[None, :], w_in=w.astype(BF16), gcat=gcat, bd=bd,
                sink=sink_b * LOG2E, out_norm_a=out_norm_a[None, :], out_norm_b=out_norm_b[None, :],
                w_o=w_o.astype(BF16), norm_mlp=norm_mlp[None, :],
                w_up=w_up.astype(BF16), w_down=w_down.astype(BF16))


def kernel(x_prompt, x_sample, norm_attn, w_in, q_norm_a, k_norm_a, q_norm_b, k_norm_b, sink_b,
           out_norm_a, out_norm_b, w_o, norm_mlp, w_up, w_down):
    yp, ys = x_prompt, x_sample
    for l in range(w_in.shape[0]):
        p = _prepare(norm_attn[l], w_in[l], q_norm_a[l], k_norm_a[l], q_norm_b[l], k_norm_b[l],
                     sink_b[l], out_norm_a[l], out_norm_b[l], w_o[l], norm_mlp[l], w_up[l],
                     w_down[l])
        yp = _layer(yp, p)
        ys = _layer(ys, p)
    return (yp, ys)
```

```python
import functools

import jax
import jax.numpy as jnp
from jax import lax
from jax.experimental import pallas as pl
from jax.experimental.pallas import tpu as pltpu

HEAD_DIM = 64
PAIR = 2 * HEAD_DIM
DILATED_BRANCHES = ((128, 1), (512, 4), (2048, 16))
WINDOW_B = 128
ROPE_THETA = 500000.0
ROPE_DIM = HEAD_DIM // 4
ROPE_HALF = ROPE_DIM // 2
EPS = 1e-6
NEG_BIG = -1e30
LOG2E = 1.4426950408889634
LANES = 128
V7X_VMEM_BYTES = 64 * 1024 * 1024

F32 = jnp.float32
BF16 = jnp.bfloat16


def _nbytes(shape, dtype):
    n = jnp.dtype(dtype).itemsize
    for s in shape:
        n *= s
    return n


def _vmem_limit(pipelined, resident, scratch, values):
    return min(2 * pipelined + resident + scratch + values, V7X_VMEM_BYTES)


def _dot(a, b):
    return jnp.dot(a, b, preferred_element_type=F32)


def _dot_nt(a, b):
    return lax.dot_general(a, b, (((1,), (1,)), ((), ())), preferred_element_type=F32)


def _proj_kernel(x_ref, g_ref, w_ref, gcat_ref, bd_ref, cos_ref, sin_ref,
                 qa1, ka1, va1, qa4, ka4, va4, qa16, ka16, va16, qb, kb, vb,
                 stage_ref, stage4_ref, *, width_a, width_b, kv_b):
    tm = x_ref.shape[0]
    x = x_ref[...]
    ms = jnp.mean(x * x, axis=-1, keepdims=True)
    xn = (x * lax.rsqrt(ms + EPS) * g_ref[...]).astype(BF16)
    cos = cos_ref[...]
    sin = sin_ref[...]
    first_half = lax.broadcasted_iota(jnp.int32, (tm, LANES), 1) % HEAD_DIM < ROPE_HALF
    bd = bd_ref[...]

    def head_norm_rope(h, col0):
        width = h.shape[1]
        sq = (h * h).astype(BF16)
        blocks = []
        sub = min(width, bd.shape[0])
        for c in range(0, width, sub):
            msq = _dot(sq[:, c:c + sub], bd[:sub, :sub])
            y = h[:, c:c + sub] * lax.rsqrt(msq + EPS) * gcat_ref[:, col0 + c:col0 + c + sub]
            for b in range(0, sub, LANES):
                yb = y[:, b:b + LANES]
                up = pltpu.roll(yb, LANES - ROPE_HALF, 1)
                dn = pltpu.roll(yb, ROPE_HALF, 1)
                blocks.append(yb * cos + jnp.where(first_half, up, dn) * sin)
        return blocks

    def write_dilated(blocks, outs, slab0):
        o1, o4, o16 = outs
        n4, n16 = tm // 4, tm // 16
        for p, blk in enumerate(blocks):
            slab = slab0 + p
            o1[p, 0] = blk.astype(BF16)
            stage_ref[slab] = blk
            for r4 in range(4):
                cls = stage_ref[slab, pl.ds(r4, n4, stride=4), :]
                o4[p, r4] = cls.astype(BF16)
                stage4_ref[slab, r4 * n4:(r4 + 1) * n4, :] = cls
            for r4 in range(4):
                for c in range(4):
                    sub = stage4_ref[slab, pl.ds(r4 * n4 + c, n16, stride=4), :]
                    o16[p, 4 * c + r4] = sub.astype(BF16)

    def write_both_halves(blk, out):
        lo = lax.broadcasted_iota(jnp.int32, (tm, LANES), 1) < HEAD_DIM
        sw = pltpu.roll(blk, HEAD_DIM, 1)
        out[0] = jnp.where(lo, blk, sw).astype(BF16)
        out[1] = jnp.where(lo, sw, blk).astype(BF16)

    def lane_blocks(h):
        return [h[:, b:b + LANES] for b in range(0, h.shape[1], LANES)]

    def write_qb(blocks):
        for p, blk in enumerate(blocks):
            qb[p] = blk.astype(BF16)

    def write_kv_b(h, c):
        write_both_halves(head_norm_rope(h[:, :kv_b], c)[0], kb)
        write_both_halves(h[:, kv_b:], vb)

    n_pa = width_a // LANES
    chunks = [
        (width_a, lambda h, c: write_dilated(head_norm_rope(h, c), (qa1, qa4, qa16), 0)),
        (width_a, lambda h, c: write_dilated(head_norm_rope(h, c), (ka1, ka4, ka16), n_pa)),
        (width_b, lambda h, c: write_qb(head_norm_rope(h, c))),
        (width_a, lambda h, c: write_dilated(lane_blocks(h), (va1, va4, va16), 2 * n_pa)),
        (2 * kv_b, write_kv_b),
    ]
    starts = [sum(w for w, _ in chunks[:i]) for i in range(len(chunks))]

    def project(i):
        return _dot(xn, w_ref[:, starts[i]:starts[i] + chunks[i][0]])

    h_next = project(0)
    for i, (_, consume) in enumerate(chunks):
        h = h_next
        if i + 1 < len(chunks):
            h_next = project(i + 1)
        consume(h, starts[i])


def _proj_call(x, g, w, gcat, bd, cos, sin, *, width_a, width_b, kv_b, tm):
    B, S, D = x.shape
    n_pa = width_a // PAIR
    n_pb = width_b // PAIR
    n_kv = kv_b // HEAD_DIM
    cols = w.shape[1]

    def dil_shape(d):
        return jax.ShapeDtypeStruct((B, n_pa, d, S // d, PAIR), BF16)

    def dil_spec(d):
        return pl.BlockSpec((None, n_pa, d, tm // d, PAIR), lambda b, i: (b, 0, 0, i, 0))

    const = lambda b, i: (0, 0)
    out_shape = []
    out_specs = []
    for d in (1, 4, 16):
        out_shape += [dil_shape(d)] * 3
        out_specs += [dil_spec(d)] * 3
    out_shape += [jax.ShapeDtypeStruct((B, n_pb, S, PAIR), BF16),
                  jax.ShapeDtypeStruct((B, n_kv, S, PAIR), BF16),
                  jax.ShapeDtypeStruct((B, n_kv, S, PAIR), BF16)]
    out_specs += [pl.BlockSpec((None, n_pb, tm, PAIR), lambda b, i: (b, 0, i, 0)),
                  pl.BlockSpec((None, n_kv, tm, PAIR), lambda b, i: (b, 0, i, 0)),
                  pl.BlockSpec((None, n_kv, tm, PAIR), lambda b, i: (b, 0, i, 0))]
    tab_spec = pl.BlockSpec((tm, LANES), lambda b, i: (i, 0))
    out_cols = 9 * width_a + width_b + 2 * n_kv * PAIR
    vmem = _vmem_limit(
        pipelined=_nbytes((tm, D), F32) + 2 * _nbytes((tm, LANES), F32)
        + _nbytes((tm, out_cols), BF16),
        resident=_nbytes(w.shape, BF16),
        scratch=2 * _nbytes((3 * n_pa, tm, LANES), F32),
        values=_nbytes((tm, D), BF16) + 6 * _nbytes((tm, width_a), F32))
    return pl.pallas_call(
        functools.partial(_proj_kernel, width_a=width_a, width_b=width_b, kv_b=kv_b),
        grid=(B, S // tm),
        in_specs=[pl.BlockSpec((None, tm, D), lambda b, i: (b, i, 0)),
                  pl.BlockSpec((1, D), const),
                  pl.BlockSpec((D, cols), const, pipeline_mode=pl.Buffered(1)),
                  pl.BlockSpec((1, gcat.shape[1]), const),
                  pl.BlockSpec(bd.shape, const),
                  tab_spec, tab_spec],
        out_specs=out_specs,
        out_shape=out_shape,
        scratch_shapes=[pltpu.VMEM((3 * n_pa, tm, LANES), F32)] * 2,
        compiler_params=pltpu.CompilerParams(
            dimension_semantics=("parallel", "parallel"), vmem_limit_bytes=vmem),
        name="proj_qkv",
    )(x, g, w, gcat, bd, cos, sin)


def _attend(blocks, sinks, finish):
    bq = blocks[0][0][0].shape[0]
    lo = lax.broadcasted_iota(jnp.int32, (bq, PAIR), 1) < HEAD_DIM
    zero = jnp.zeros((bq, PAIR), BF16)

    def scores(i):
        q_tiles, k, _, _ = blocks[i]
        rows = []
        for q in q_tiles:
            rows.append(jnp.where(lo, q, zero))
            rows.append(jnp.where(lo, zero, q))
        return _dot_nt(jnp.concatenate(rows, axis=0), k)

    s_next = scores(0)
    for i, (q_tiles, _, v, bias) in enumerate(blocks):
        s = s_next
        if i + 1 < len(blocks):
            s_next = scores(i + 1)
        ps, ms = [], []
        for h in range(2 * len(q_tiles)):
            sh = s[h * bq:(h + 1) * bq] + bias
            m = jnp.max(sh, axis=-1, keepdims=True)
            if sinks is not None:
                m = jnp.maximum(m, sinks[h])
            ps.append(jnp.exp2(sh - m).astype(BF16))
            ms.append(m)
        pv = _dot(jnp.concatenate(ps, axis=0),
                  jnp.concatenate([v, jnp.ones_like(v)], axis=1))
        res = []
        for t in range(len(q_tiles)):
            a, b = 2 * t, 2 * t + 1
            pa, pb = pv[a * bq:(a + 1) * bq], pv[b * bq:(b + 1) * bq]
            la, lb = pa[:, PAIR:], pb[:, PAIR:]
            if sinks is not None:
                la = la + jnp.exp2(sinks[a] - ms[a])
                lb = lb + jnp.exp2(sinks[b] - ms[b])
            res.append((jnp.where(lo, pa[:, :PAIR], pb[:, :PAIR]),
                        jnp.where(lo, ms[a], ms[b]), jnp.where(lo, la, lb)))
        finish(i, res)


def _band_mask(bq, tk, offset, half_window):
    r = lax.broadcasted_iota(jnp.int32, (bq, tk), 0)
    c = lax.broadcasted_iota(jnp.int32, (bq, tk), 1)
    return jnp.abs(c - r + offset) <= half_window


def _branch_cfg(S, window, d):
    hw = window // (2 * d)
    seg = S // d
    bq = min(2 * hw, seg)
    tk = min(bq + 2 * hw, seg)
    return hw, seg, bq, tk


def _fill_bias(bias_ref, bq, tk, hw):
    for n in range(bias_ref.shape[0]):
        bias_ref[n] = jnp.where(_band_mask(bq, tk, -n * hw, hw), 0.0, NEG_BIG)


def _attn_a_kernel(q1, k1, v1, q4, k4, v4, q16, k16, v16, o_ref, st1_ref, st16_ref,
                   *bias_refs, group):
    S = o_ref.shape[0]
    S4 = S // 4
    refs = {1: (q1, k1, v1), 4: (q4, k4, v4), 16: (q16, k16, v16)}
    assert [d for _, d in DILATED_BRANCHES] == [1, 4, 16]
    cfg = {}
    for bi, (window, d) in enumerate(DILATED_BRANCHES):
        cfg[d] = _branch_cfg(S, window, d) + (bias_refs[bi],)
        hw, seg, bq, tk, bias_ref = cfg[d]
        _fill_bias(bias_ref, bq, tk, hw)
    bq = cfg[1][2]
    assert cfg[4][2] == bq and cfg[16][2] == bq

    def block(d, i0):
        hw, seg, _, tk, bias_ref = cfg[d]
        q_ref, k_ref, v_ref = refs[d]
        r = i0 // seg
        m0 = i0 - r * seg
        back = m0 - jnp.clip(m0 - hw, 0, seg - tk)
        ks = pl.multiple_of(i0 - back, hw)
        return (([q_ref[pl.ds(i0, bq), :]], k_ref[pl.ds(ks, tk), :], v_ref[pl.ds(ks, tk), :],
                 bias_ref[back // hw]), (r, m0))

    def store_step(j, carry):
        blocks, dests = [], []
        for g in range(group):
            i0 = pl.multiple_of((j * group + g) * bq, bq)
            for d in (1, 16):
                blk, (r, m0) = block(d, i0)
                blocks.append(blk)
                if d == 1:
                    dests.append((st1_ref, pl.ds(i0, bq)))
                else:
                    dests.append((st16_ref, pl.ds((r % 4) * S4 + 4 * m0 + r // 4, bq, stride=4)))

        def finish(i, res):
            ref, rows = dests[i]
            for a, val in enumerate(res[0]):
                ref[a, rows, :] = val

        _attend(blocks, None, finish)
        return carry

    lax.fori_loop(0, S // (bq * group), store_step, 0)

    def merge_step(j, carry):
        blocks, rows = [], []
        for g in range(group):
            i0 = pl.multiple_of((j * group + g) * bq, bq)
            blk, (r, m0) = block(4, i0)
            blocks.append(blk)
            rows.append((pl.ds(i0, bq), pl.ds(r + 4 * m0, bq, stride=4)))

        def finish(g, res):
            own, tokens = rows[g]
            ((acc4, m4, l4),) = res
            acc1, m1, l1 = [st1_ref[a, tokens, :] for a in range(3)]
            acc16, m16, l16 = [st16_ref[a, own, :] for a in range(3)]
            m = jnp.maximum(jnp.maximum(m1, m4), m16)
            a1, a4, a16 = jnp.exp2(m1 - m), jnp.exp2(m4 - m), jnp.exp2(m16 - m)
            acc = a1 * acc1 + a4 * acc4 + a16 * acc16
            l = a1 * l1 + a4 * l4 + a16 * l16
            o_ref[tokens, :] = acc / l

        _attend(blocks, None, finish)
        return carry

    lax.fori_loop(0, S // (bq * group), merge_step, 0)


def _attn_a_call(qkv, B, S, width_a, *, group):
    n_p = width_a // PAIR
    flat = [t.reshape(B, n_p, S, PAIR) for t in qkv]
    spec = pl.BlockSpec((None, None, S, PAIR), lambda b, p: (b, p, 0, 0))
    scratch = [(3, S, PAIR)] * 2
    for window, d in DILATED_BRANCHES:
        hw, seg, bq, tk = _branch_cfg(S, window, d)
        scratch.append(((tk - bq) // hw + 1, bq, tk))
    vmem = _vmem_limit(
        pipelined=9 * _nbytes((S, PAIR), BF16) + _nbytes((S, PAIR), F32),
        resident=0,
        scratch=sum(_nbytes(s, F32) for s in scratch),
        values=2 * group * (_nbytes((2 * bq, 2 * bq), F32) + _nbytes((2 * bq, 2 * bq), BF16)
                            + _nbytes((2 * bq, 2 * PAIR), F32)))
    return pl.pallas_call(
        functools.partial(_attn_a_kernel, group=group),
        grid=(B, n_p),
        in_specs=[spec] * 9,
        out_specs=pl.BlockSpec((None, S, PAIR), lambda b, p: (b, 0, p)),
        out_shape=jax.ShapeDtypeStruct((B, S, width_a), F32),
        scratch_shapes=[pltpu.VMEM(s, F32) for s in scratch],
        compiler_params=pltpu.CompilerParams(
            dimension_semantics=("parallel", "parallel"), vmem_limit_bytes=vmem),
        name="attn_dilated",
    )(*flat)


def _attn_b_kernel(sink_ref, q_ref, k_ref, v_ref, o_ref, bias_ref, *, group):
    n_t, S, _ = q_ref.shape
    kv_head = pl.program_id(1)
    bq = WINDOW_B
    tk = min(3 * bq, S)
    sinks = [sink_ref[kv_head * 2 * n_t + h] for h in range(2 * n_t)]
    _fill_bias(bias_ref, bq, tk, WINDOW_B)

    def step(j, carry):
        blocks, starts = [], []
        for g in range(group):
            i0 = pl.multiple_of((j * group + g) * bq, bq)
            ks = pl.multiple_of(jnp.clip(i0 - WINDOW_B, 0, S - tk), WINDOW_B)
            blocks.append(([q_ref[t, pl.ds(i0, bq), :] for t in range(n_t)],
                           k_ref[pl.ds(ks, tk), :], v_ref[pl.ds(ks, tk), :],
                           bias_ref[(i0 - ks) // WINDOW_B]))
            starts.append(i0)
        def finish(g, res):
            for t, (acc, m, l) in enumerate(res):
                o_ref[pl.ds(starts[g], bq), t * PAIR:(t + 1) * PAIR] = acc / l

        _attend(blocks, sinks, finish)
        return carry

    lax.fori_loop(0, S // (bq * group), step, 0)


def _attn_b_call(sink, qb, kb, vb, *, group):
    B, n_p, S, _ = qb.shape
    n_kv = kb.shape[1]
    n_t = n_p // n_kv
    kv_spec = pl.BlockSpec((None, None, S, PAIR), lambda b, g: (b, g, 0, 0))
    bq = WINDOW_B
    tk = min(3 * bq, S)
    bias_shape = ((tk - bq) // WINDOW_B + 1, bq, tk)
    rows = 2 * n_t * bq
    vmem = _vmem_limit(
        pipelined=(n_t + 2) * _nbytes((S, PAIR), BF16) + _nbytes((S, n_t * PAIR), F32),
        resident=0,
        scratch=_nbytes(bias_shape, F32),
        values=group * (_nbytes((rows, tk), F32) + _nbytes((rows, tk), BF16)
                        + _nbytes((rows, 2 * PAIR), F32)))
    return pl.pallas_call(
        functools.partial(_attn_b_kernel, group=group),
        grid=(B, n_kv),
        in_specs=[pl.BlockSpec(memory_space=pltpu.SMEM),
                  pl.BlockSpec((None, n_t, S, PAIR), lambda b, g: (b, g, 0, 0)),
                  kv_spec, kv_spec],
        out_specs=pl.BlockSpec((None, S, n_t * PAIR), lambda b, g: (b, 0, g)),
        out_shape=jax.ShapeDtypeStruct((B, S, n_p * PAIR), F32),
        scratch_shapes=[pltpu.VMEM(bias_shape, F32)],
        compiler_params=pltpu.CompilerParams(
            dimension_semantics=("parallel", "parallel"), vmem_limit_bytes=vmem),
        name="attn_window",
    )(sink, qb, kb, vb)


def _rms(x, g):
    ms = jnp.mean(x * x, axis=-1, keepdims=True)
    return x * lax.rsqrt(ms + EPS) * g


def _mlp_kernel(oa_ref, ob_ref, x_ref, ga_ref, gb_ref, wo_ref, gm_ref, wu_ref, wd_ref, y_ref,
                *, ff_chunk):
    wa = oa_ref.shape[1]
    na = _rms(oa_ref[...], ga_ref[...]).astype(BF16)
    nb = _rms(ob_ref[...], gb_ref[...]).astype(BF16)
    h = x_ref[...] + _dot(na, wo_ref[:wa, :]) + _dot(nb, wo_ref[wa:, :])
    hn = _rms(h, gm_ref[...]).astype(BF16)
    acc = None
    for c in range(0, wu_ref.shape[1], ff_chunk):
        u = jnp.maximum(_dot(hn, wu_ref[:, c:c + ff_chunk]), 0.0)
        d = _dot((u * u).astype(BF16), wd_ref[c:c + ff_chunk, :])
        acc = d if acc is None else acc + d
    y_ref[...] = h + acc


def _mlp_call(oa, ob, x, ga, gb, wo, gm, wu, wd, *, tm, ff_chunk):
    B, S, D = x.shape
    const = lambda b, i: (0, 0)

    def resident(a):
        return pl.BlockSpec(a.shape, const, pipeline_mode=pl.Buffered(1))

    def rows(a):
        return pl.BlockSpec((None, tm, a.shape[2]), lambda b, i: (b, i, 0))

    vmem = _vmem_limit(
        pipelined=_nbytes((tm, oa.shape[2] + ob.shape[2] + 2 * D), F32),
        resident=sum(_nbytes(a.shape, a.dtype) for a in (wo, wu, wd)),
        scratch=0,
        values=2 * _nbytes((tm, D), F32) + _nbytes((tm, D), BF16)
        + _nbytes((tm, ff_chunk), F32) + _nbytes((tm, ff_chunk), BF16))
    return pl.pallas_call(
        functools.partial(_mlp_kernel, ff_chunk=ff_chunk),
        grid=(B, S // tm),
        in_specs=[rows(oa), rows(ob), rows(x), resident(ga), resident(gb), resident(wo),
                  resident(gm), resident(wu), resident(wd)],
        out_specs=rows(x),
        out_shape=jax.ShapeDtypeStruct((B, S, D), F32),
        compiler_params=pltpu.CompilerParams(
            dimension_semantics=("parallel", "parallel"), vmem_limit_bytes=vmem),
        name="out_mlp",
    )(oa, ob, x, ga, gb, wo, gm, wu, wd)


def _rope_tables(S):
    inv = ROPE_THETA ** (-jnp.arange(ROPE_HALF, dtype=F32) * 2.0 / ROPE_DIM)
    ang = jnp.arange(S, dtype=F32)[:, None] * inv[None, :]
    cos, sin = jnp.cos(ang), jnp.sin(ang)
    ones = jnp.ones((S, HEAD_DIM - ROPE_DIM), F32)
    zeros = jnp.zeros((S, HEAD_DIM - ROPE_DIM), F32)
    head_cos = jnp.concatenate([cos, cos, ones], axis=1)
    head_sin = jnp.concatenate([-sin, sin, zeros], axis=1)
    rep = LANES // HEAD_DIM
    return tuple(jnp.tile(t, (1, rep)) for t in (head_cos, head_sin))


TILES = dict(tm_proj=512, tm_mlp=1024, ff_chunk=512, group_a=16, group_b=16)


def _layer(x, p):
    B, S, D = x.shape
    width_a, width_b, kv_b = p["width_a"], p["width_b"], p["kv_b"]
    cos, sin = _rope_tables(S)
    outs = _proj_call(x, p["norm_attn"], p["w_in"], p["gcat"], p["bd"], cos, sin,
                      width_a=width_a, width_b=width_b, kv_b=kv_b, tm=TILES["tm_proj"])
    qa1, ka1, va1, qa4, ka4, va4, qa16, ka16, va16, qb, kb, vb = outs
    oa = _attn_a_call((qa1, ka1, va1, qa4, ka4, va4, qa16, ka16, va16), B, S, width_a,
                      group=TILES["group_a"])
    ob = _attn_b_call(p["sink"], qb, kb, vb, group=TILES["group_b"])
    return _mlp_call(oa, ob, x, p["out_norm_a"], p["out_norm_b"], p["w_o"], p["norm_mlp"],
                     p["w_up"], p["w_down"], tm=TILES["tm_mlp"], ff_chunk=TILES["ff_chunk"])


def _prepare(norm_attn, w_in, q_norm_a, k_norm_a, q_norm_b, k_norm_b, sink_b, out_norm_a,
             out_norm_b, w_o, norm_mlp, w_up, w_down):
    width_a = out_norm_a.shape[0]
    width_b = out_norm_b.shape[0]
    kv_b = (w_in.shape[1] - 3 * width_a - width_b) // 2
    o_qa, o_ka, o_va = 0, width_a, 2 * width_a
    o_qb = 3 * width_a
    o_kb = o_qb + width_b
    o_vb = o_kb + kv_b
    w = jnp.concatenate([w_in[:, o_qa:o_qa + width_a], w_in[:, o_ka:o_ka + width_a],
                         w_in[:, o_qb:o_qb + width_b], w_in[:, o_va:o_va + width_a],
                         w_in[:, o_kb:o_kb + kv_b], w_in[:, o_vb:o_vb + kv_b]], axis=1)
    scale = HEAD_DIM ** -0.5 * LOG2E
    gcat = jnp.concatenate([jnp.tile(q_norm_a, width_a // HEAD_DIM) * scale,
                            jnp.tile(k_norm_a, width_a // HEAD_DIM),
                            jnp.tile(q_norm_b, width_b // HEAD_DIM) * scale,
                            jnp.ones((width_a,), F32),
                            jnp.tile(k_norm_b, kv_b // HEAD_DIM)])[None, :]
    n = 2 * LANES
    same_head = jnp.arange(n)[:, None] // HEAD_DIM == jnp.arange(n)[None, :] // HEAD_DIM
    bd = (same_head.astype(F32) / HEAD_DIM).astype(BF16)
    return dict(width_a=width_a, width_b=width_b, kv_b=kv_b,
                norm_attn=norm_attn[None, :], w_in=w.astype(BF16), gcat=gcat, bd=bd,
                sink=sink_b * LOG2E, out_norm_a=out_norm_a[None, :], out_norm_b=out_norm_b[None, :],
                w_o=w_o.astype(BF16), norm_mlp=norm_mlp[None, :],
                w_up=w_up.astype(BF16), w_down=w_down.astype(BF16))


def kernel(x_prompt, x_sample, norm_attn, w_in, q_norm_a, k_norm_a, q_norm_b, k_norm_b, sink_b,
           out_norm_a, out_norm_b, w_o, norm_mlp, w_up, w_down):
    yp, ys = x_prompt, x_sample
    for l in range(w_in.shape[0]):
        p = _prepare(norm_attn[l], w_in[l], q_norm_a[l], k_norm_a[l], q_norm_b[l], k_norm_b[l],
                     sink_b[l], out_norm_a[l], out_norm_b[l], w_o[l], norm_mlp[l], w_up[l],
                     w_down[l])
        yp = _layer(yp, p)
        ys = _layer(ys, p)
    return (yp, ys)
```

```python
import functools

import jax
import jax.numpy as jnp
from jax import lax
from jax.experimental import pallas as pl
from jax.experimental.pallas import tpu as pltpu

HEAD_DIM = 64
PAIR = 2 * HEAD_DIM
DILATED_BRANCHES = ((128, 1), (512, 4), (2048, 16))
WINDOW_B = 128
ROPE_THETA = 500000.0
ROPE_DIM = HEAD_DIM // 4
ROPE_HALF = ROPE_DIM // 2
EPS = 1e-6
NEG_BIG = -1e30
LOG2E = 1.4426950408889634
LANES = 128
V7X_VMEM_BYTES = 64 * 1024 * 1024

F32 = jnp.float32
BF16 = jnp.bfloat16


def _nbytes(shape, dtype):
    n = jnp.dtype(dtype).itemsize
    for s in shape:
        n *= s
    return n


def _vmem_limit(pipelined, resident, scratch, values):
    return min(2 * pipelined + resident + scratch + values, V7X_VMEM_BYTES)


def _dot(a, b):
    return jnp.dot(a, b, preferred_element_type=F32)


def _dot_nt(a, b):
    return lax.dot_general(a, b, (((1,), (1,)), ((), ())), preferred_element_type=F32)


def _proj_kernel(x_ref, g_ref, w_ref, gcat_ref, bd_ref, cos_ref, sin_ref,
                 qa1, ka1, va1, qa4, ka4, va4, qa16, ka16, va16, qb, kb, vb,
                 stage_ref, stage4_ref, *, width_a, width_b, kv_b):
    tm = x_ref.shape[0]
    x = x_ref[...]
    ms = jnp.mean(x * x, axis=-1, keepdims=True)
    xn = (x * lax.rsqrt(ms + EPS) * g_ref[...]).astype(BF16)
    cos = cos_ref[...]
    sin = sin_ref[...]
    first_half = lax.broadcasted_iota(jnp.int32, (tm, LANES), 1) % HEAD_DIM < ROPE_HALF
    bd = bd_ref[...]

    def head_norm_rope(h, col0):
        width = h.shape[1]
        sq = (h * h).astype(BF16)
        blocks = []
        sub = min(width, bd.shape[0])
        for c in range(0, width, sub):
            msq = _dot(sq[:, c:c + sub], bd[:sub, :sub])
            y = h[:, c:c + sub] * lax.rsqrt(msq + EPS) * gcat_ref[:, col0 + c:col0 + c + sub]
            for b in range(0, sub, LANES):
                yb = y[:, b:b + LANES]
                up = pltpu.roll(yb, LANES - ROPE_HALF, 1)
                dn = pltpu.roll(yb, ROPE_HALF, 1)
                blocks.append(yb * cos + jnp.where(first_half, up, dn) * sin)
        return blocks

    def write_dilated(blocks, outs, slab0):
        o1, o4, o16 = outs
        n4, n16 = tm // 4, tm // 16
        for p, blk in enumerate(blocks):
            slab = slab0 + p
            o1[p, 0] = blk.astype(BF16)
            stage_ref[slab] = blk
            for r4 in range(4):
                cls = stage_ref[slab, pl.ds(r4, n4, stride=4), :]
                o4[p, r4] = cls.astype(BF16)
                stage4_ref[slab, r4 * n4:(r4 + 1) * n4, :] = cls
            for r4 in range(4):
                for c in range(4):
                    sub = stage4_ref[slab, pl.ds(r4 * n4 + c, n16, stride=4), :]
                    o16[p, 4 * c + r4] = sub.astype(BF16)

    def write_both_halves(blk, out):
        lo = lax.broadcasted_iota(jnp.int32, (tm, LANES), 1) < HEAD_DIM
        sw = pltpu.roll(blk, HEAD_DIM, 1)
        out[0] = jnp.where(lo, blk, sw).astype(BF16)
        out[1] = jnp.where(lo, sw, blk).astype(BF16)

    def lane_blocks(h):
        return [h[:, b:b + LANES] for b in range(0, h.shape[1], LANES)]

    def write_qb(blocks):
        for p, blk in enumerate(blocks):
            qb[p] = blk.astype(BF16)

    def write_kv_b(h, c):
        write_both_halves(head_norm_rope(h[:, :kv_b], c)[0], kb)
        write_both_halves(h[:, kv_b:], vb)

    n_pa = width_a // LANES
    chunks = [
        (width_a, lambda h, c: write_dilated(head_norm_rope(h, c), (qa1, qa4, qa16), 0)),
        (width_a, lambda h, c: write_dilated(head_norm_rope(h, c), (ka1, ka4, ka16), n_pa)),
        (width_b, lambda h, c: write_qb(head_norm_rope(h, c))),
        (width_a, lambda h, c: write_dilated(lane_blocks(h), (va1, va4, va16), 2 * n_pa)),
        (2 * kv_b, write_kv_b),
    ]
    starts = [sum(w for w, _ in chunks[:i]) for i in range(len(chunks))]

    def project(i):
        return _dot(xn, w_ref[:, starts[i]:starts[i] + chunks[i][0]])

    h_next = project(0)
    for i, (_, consume) in enumerate(chunks):
        h = h_next
        if i + 1 < len(chunks):
            h_next = project(i + 1)
        consume(h, starts[i])


def _proj_call(x, g, w, gcat, bd, cos, sin, *, width_a, width_b, kv_b, tm):
    B, S, D = x.shape
    n_pa = width_a // PAIR
    n_pb = width_b // PAIR
    n_kv = kv_b // HEAD_DIM
    cols = w.shape[1]
    bf16_rows = 16
    assert [d for _, d in DILATED_BRANCHES] == [1, 4, 16]
    assert S % tm == 0 and tm % (16 * bf16_rows) == 0, (S, tm)

    def dil_shape(d):
        return jax.ShapeDtypeStruct((B, n_pa, d, S // d, PAIR), BF16)

    def dil_spec(d):
        return pl.BlockSpec((None, n_pa, d, tm // d, PAIR), lambda b, i: (b, 0, 0, i, 0))

    const = lambda b, i: (0, 0)
    out_shape = []
    out_specs = []
    for d in (1, 4, 16):
        out_shape += [dil_shape(d)] * 3
        out_specs += [dil_spec(d)] * 3
    out_shape += [jax.ShapeDtypeStruct((B, n_pb, S, PAIR), BF16),
                  jax.ShapeDtypeStruct((B, n_kv, S, PAIR), BF16),
                  jax.ShapeDtypeStruct((B, n_kv, S, PAIR), BF16)]
    out_specs += [pl.BlockSpec((None, n_pb, tm, PAIR), lambda b, i: (b, 0, i, 0)),
                  pl.BlockSpec((None, n_kv, tm, PAIR), lambda b, i: (b, 0, i, 0)),
                  pl.BlockSpec((None, n_kv, tm, PAIR), lambda b, i: (b, 0, i, 0))]
    tab_spec = pl.BlockSpec((tm, LANES), lambda b, i: (i, 0))
    out_cols = 9 * width_a + width_b + 2 * n_kv * PAIR
    vmem = _vmem_limit(
        pipelined=_nbytes((tm, D), F32) + 2 * _nbytes((tm, LANES), F32)
        + _nbytes((tm, out_cols), BF16),
        resident=_nbytes(w.shape, BF16),
        scratch=2 * _nbytes((3 * n_pa, tm, LANES), F32),
        values=_nbytes((tm, D), BF16) + 6 * _nbytes((tm, width_a), F32))
    return pl.pallas_call(
        functools.partial(_proj_kernel, width_a=width_a, width_b=width_b, kv_b=kv_b),
        grid=(B, S // tm),
        in_specs=[pl.BlockSpec((None, tm, D), lambda b, i: (b, i, 0)),
                  pl.BlockSpec((1, D), const),
                  pl.BlockSpec((D, cols), const, pipeline_mode=pl.Buffered(1)),
                  pl.BlockSpec((1, gcat.shape[1]), const),
                  pl.BlockSpec(bd.shape, const),
                  tab_spec, tab_spec],
        out_specs=out_specs,
        out_shape=out_shape,
        scratch_shapes=[pltpu.VMEM((3 * n_pa, tm, LANES), F32)] * 2,
        compiler_params=pltpu.CompilerParams(
            dimension_semantics=("parallel", "parallel"), vmem_limit_bytes=vmem),
        name="proj_qkv",
    )(x, g, w, gcat, bd, cos, sin)


BLOCKS_IN_FLIGHT = 3


def _attend(blocks, sinks, finish):
    bq = blocks[0][0][0].shape[0]
    lo = lax.broadcasted_iota(jnp.int32, (bq, PAIR), 1) < HEAD_DIM
    zero = jnp.zeros((bq, PAIR), BF16)

    def scores(i):
        q_tiles, k, _, _ = blocks[i]
        rows = []
        for q in q_tiles:
            rows.append(jnp.where(lo, q, zero))
            rows.append(jnp.where(lo, zero, q))
        return _dot_nt(jnp.concatenate(rows, axis=0), k)

    s_next = scores(0)
    for i, (q_tiles, _, v, bias) in enumerate(blocks):
        s = s_next
        if i + 1 < len(blocks):
            s_next = scores(i + 1)
        ps, ms = [], []
        for h in range(2 * len(q_tiles)):
            sh = s[h * bq:(h + 1) * bq] + bias
            m = jnp.max(sh, axis=-1, keepdims=True)
            if sinks is not None:
                m = jnp.maximum(m, sinks[h])
            ps.append(jnp.exp2(sh - m).astype(BF16))
            ms.append(m)
        pv = _dot(jnp.concatenate(ps, axis=0),
                  jnp.concatenate([v, jnp.ones_like(v)], axis=1))
        res = []
        for t in range(len(q_tiles)):
            a, b = 2 * t, 2 * t + 1
            pa, pb = pv[a * bq:(a + 1) * bq], pv[b * bq:(b + 1) * bq]
            la, lb = pa[:, PAIR:], pb[:, PAIR:]
            if sinks is not None:
                la = la + jnp.exp2(sinks[a] - ms[a])
                lb = lb + jnp.exp2(sinks[b] - ms[b])
            res.append((jnp.where(lo, pa[:, :PAIR], pb[:, :PAIR]),
                        jnp.where(lo, ms[a], ms[b]), jnp.where(lo, la, lb)))
        finish(i, res)


def _band_mask(bq, tk, offset, half_window):
    r = lax.broadcasted_iota(jnp.int32, (bq, tk), 0)
    c = lax.broadcasted_iota(jnp.int32, (bq, tk), 1)
    return jnp.abs(c - r + offset) <= half_window


def _branch_cfg(S, window, d):
    hw = window // (2 * d)
    seg = S // d
    bq = min(2 * hw, seg)
    tk = min(bq + 2 * hw, seg)
    return hw, seg, bq, tk


def _fill_bias(bias_ref, bq, tk, hw):
    for n in range(bias_ref.shape[0]):
        bias_ref[n] = jnp.where(_band_mask(bq, tk, -n * hw, hw), 0.0, NEG_BIG)


def _attn_a_kernel(q1, k1, v1, q4, k4, v4, q16, k16, v16, o_ref, st1_ref, st16_ref,
                   *bias_refs, group):
    S = o_ref.shape[0]
    S4 = S // 4
    refs = {1: (q1, k1, v1), 4: (q4, k4, v4), 16: (q16, k16, v16)}
    assert [d for _, d in DILATED_BRANCHES] == [1, 4, 16]
    cfg = {}
    for bi, (window, d) in enumerate(DILATED_BRANCHES):
        cfg[d] = _branch_cfg(S, window, d) + (bias_refs[bi],)
        hw, seg, bq, tk, bias_ref = cfg[d]
        _fill_bias(bias_ref, bq, tk, hw)
    bq = cfg[1][2]
    assert cfg[4][2] == bq and cfg[16][2] == bq

    def block(d, i0):
        hw, seg, _, tk, bias_ref = cfg[d]
        q_ref, k_ref, v_ref = refs[d]
        r = i0 // seg
        m0 = i0 - r * seg
        back = m0 - jnp.clip(m0 - hw, 0, seg - tk)
        ks = pl.multiple_of(i0 - back, hw)
        return (([q_ref[pl.ds(i0, bq), :]], k_ref[pl.ds(ks, tk), :], v_ref[pl.ds(ks, tk), :],
                 bias_ref[back // hw]), (r, m0))

    def store_step(j, carry):
        blocks, dests = [], []
        for g in range(group):
            i0 = pl.multiple_of((j * group + g) * bq, bq)
            for d in (1, 16):
                blk, (r, m0) = block(d, i0)
                blocks.append(blk)
                if d == 1:
                    dests.append((st1_ref, pl.ds(i0, bq)))
                else:
                    dests.append((st16_ref, pl.ds((r % 4) * S4 + 4 * m0 + r // 4, bq, stride=4)))

        def finish(i, res):
            ref, rows = dests[i]
            for a, val in enumerate(res[0]):
                ref[a, rows, :] = val

        _attend(blocks, None, finish)
        return carry

    lax.fori_loop(0, S // (bq * group), store_step, 0)

    def merge_step(j, carry):
        blocks, rows = [], []
        for g in range(group):
            i0 = pl.multiple_of((j * group + g) * bq, bq)
            blk, (r, m0) = block(4, i0)
            blocks.append(blk)
            rows.append((pl.ds(i0, bq), pl.ds(r + 4 * m0, bq, stride=4)))

        def finish(g, res):
            own, tokens = rows[g]
            ((acc4, m4, l4),) = res
            acc1, m1, l1 = [st1_ref[a, tokens, :] for a in range(3)]
            acc16, m16, l16 = [st16_ref[a, own, :] for a in range(3)]
            m = jnp.maximum(jnp.maximum(m1, m4), m16)
            a1, a4, a16 = jnp.exp2(m1 - m), jnp.exp2(m4 - m), jnp.exp2(m16 - m)
            acc = a1 * acc1 + a4 * acc4 + a16 * acc16
            l = a1 * l1 + a4 * l4 + a16 * l16
            o_ref[tokens, :] = acc / l

        _attend(blocks, None, finish)
        return carry

    lax.fori_loop(0, S // (bq * group), merge_step, 0)


def _attn_a_call(qkv, B, S, width_a, *, group):
    n_p = width_a // PAIR
    flat = [t.reshape(B, n_p, S, PAIR) for t in qkv]
    spec = pl.BlockSpec((None, None, S, PAIR), lambda b, p: (b, p, 0, 0))
    scratch = [(3, S, PAIR)] * 2
    for window, d in DILATED_BRANCHES:
        hw, seg, bq, tk = _branch_cfg(S, window, d)
        assert S % (d * hw) == 0, (S, d, hw)
        scratch.append(((tk - bq) // hw + 1, bq, tk))
    group = min(group, S // bq)
    assert S % (bq * group) == 0, (S, bq, group)
    vmem = _vmem_limit(
        pipelined=9 * _nbytes((S, PAIR), BF16) + _nbytes((S, PAIR), F32),
        resident=0,
        scratch=sum(_nbytes(s, F32) for s in scratch),
        values=BLOCKS_IN_FLIGHT * (_nbytes((2 * bq, 2 * bq), F32)
                                   + _nbytes((2 * bq, 2 * bq), BF16)
                                   + _nbytes((2 * bq, 2 * PAIR), F32)))
    return pl.pallas_call(
        functools.partial(_attn_a_kernel, group=group),
        grid=(B, n_p),
        in_specs=[spec] * 9,
        out_specs=pl.BlockSpec((None, S, PAIR), lambda b, p: (b, 0, p)),
        out_shape=jax.ShapeDtypeStruct((B, S, width_a), F32),
        scratch_shapes=[pltpu.VMEM(s, F32) for s in scratch],
        compiler_params=pltpu.CompilerParams(
            dimension_semantics=("parallel", "parallel"), vmem_limit_bytes=vmem),
        name="attn_dilated",
    )(*flat)


def _attn_b_kernel(sink_ref, q_ref, k_ref, v_ref, o_ref, bias_ref, *, group):
    n_t, S, _ = q_ref.shape
    kv_head = pl.program_id(1)
    bq = WINDOW_B
    tk = min(3 * bq, S)
    sinks = [sink_ref[kv_head * 2 * n_t + h] for h in range(2 * n_t)]
    _fill_bias(bias_ref, bq, tk, WINDOW_B)

    def step(j, carry):
        blocks, starts = [], []
        for g in range(group):
            i0 = pl.multiple_of((j * group + g) * bq, bq)
            ks = pl.multiple_of(jnp.clip(i0 - WINDOW_B, 0, S - tk), WINDOW_B)
            blocks.append(([q_ref[t, pl.ds(i0, bq), :] for t in range(n_t)],
                           k_ref[pl.ds(ks, tk), :], v_ref[pl.ds(ks, tk), :],
                           bias_ref[(i0 - ks) // WINDOW_B]))
            starts.append(i0)
        def finish(g, res):
            for t, (acc, m, l) in enumerate(res):
                o_ref[pl.ds(starts[g], bq), t * PAIR:(t + 1) * PAIR] = acc / l

        _attend(blocks, sinks, finish)
        return carry

    lax.fori_loop(0, S // (bq * group), step, 0)


def _attn_b_call(sink, qb, kb, vb, *, group):
    B, n_p, S, _ = qb.shape
    n_kv = kb.shape[1]
    n_t = n_p // n_kv
    kv_spec = pl.BlockSpec((None, None, S, PAIR), lambda b, g: (b, g, 0, 0))
    bq = WINDOW_B
    tk = min(3 * bq, S)
    bias_shape = ((tk - bq) // WINDOW_B + 1, bq, tk)
    rows = 2 * n_t * bq
    group = min(group, S // bq)
    assert S % (bq * group) == 0, (S, bq, group)
    vmem = _vmem_limit(
        pipelined=(n_t + 2) * _nbytes((S, PAIR), BF16) + _nbytes((S, n_t * PAIR), F32),
        resident=0,
        scratch=_nbytes(bias_shape, F32),
        values=BLOCKS_IN_FLIGHT * (_nbytes((rows, tk), F32) + _nbytes((rows, tk), BF16)
                                   + _nbytes((rows, 2 * PAIR), F32)))
    return pl.pallas_call(
        functools.partial(_attn_b_kernel, group=group),
        grid=(B, n_kv),
        in_specs=[pl.BlockSpec(memory_space=pltpu.SMEM),
                  pl.BlockSpec((None, n_t, S, PAIR), lambda b, g: (b, g, 0, 0)),
                  kv_spec, kv_spec],
        out_specs=pl.BlockSpec((None, S, n_t * PAIR), lambda b, g: (b, 0, g)),
        out_shape=jax.ShapeDtypeStruct((B, S, n_p * PAIR), F32),
        scratch_shapes=[pltpu.VMEM(bias_shape, F32)],
        compiler_params=pltpu.CompilerParams(
            dimension_semantics=("parallel", "parallel"), vmem_limit_bytes=vmem),
        name="attn_window",
    )(sink, qb, kb, vb)


def _rms(x, g):
    ms = jnp.mean(x * x, axis=-1, keepdims=True)
    return x * lax.rsqrt(ms + EPS) * g


def _mlp_kernel(oa_ref, ob_ref, x_ref, ga_ref, gb_ref, wo_ref, gm_ref, wu_ref, wd_ref, y_ref,
                *, ff_chunk):
    wa = oa_ref.shape[1]
    na = _rms(oa_ref[...], ga_ref[...]).astype(BF16)
    nb = _rms(ob_ref[...], gb_ref[...]).astype(BF16)
    h = x_ref[...] + _dot(na, wo_ref[:wa, :]) + _dot(nb, wo_ref[wa:, :])
    hn = _rms(h, gm_ref[...]).astype(BF16)
    acc = None
    for c in range(0, wu_ref.shape[1], ff_chunk):
        u = jnp.maximum(_dot(hn, wu_ref[:, c:c + ff_chunk]), 0.0)
        d = _dot((u * u).astype(BF16), wd_ref[c:c + ff_chunk, :])
        acc = d if acc is None else acc + d
    y_ref[...] = h + acc


def _mlp_call(oa, ob, x, ga, gb, wo, gm, wu, wd, *, tm, ff_chunk):
    B, S, D = x.shape
    const = lambda b, i: (0, 0)

    def resident(a):
        return pl.BlockSpec(a.shape, const, pipeline_mode=pl.Buffered(1))

    def rows(a):
        return pl.BlockSpec((None, tm, a.shape[2]), lambda b, i: (b, i, 0))

    assert S % tm == 0 and wu.shape[1] % ff_chunk == 0, (S, tm, wu.shape, ff_chunk)
    vmem = _vmem_limit(
        pipelined=_nbytes((tm, oa.shape[2] + ob.shape[2] + 2 * D), F32),
        resident=sum(_nbytes(a.shape, a.dtype) for a in (wo, wu, wd)),
        scratch=0,
        values=2 * _nbytes((tm, D), F32) + _nbytes((tm, D), BF16)
        + _nbytes((tm, ff_chunk), F32) + _nbytes((tm, ff_chunk), BF16))
    return pl.pallas_call(
        functools.partial(_mlp_kernel, ff_chunk=ff_chunk),
        grid=(B, S // tm),
        in_specs=[rows(oa), rows(ob), rows(x), resident(ga), resident(gb), resident(wo),
                  resident(gm), resident(wu), resident(wd)],
        out_specs=rows(x),
        out_shape=jax.ShapeDtypeStruct((B, S, D), F32),
        compiler_params=pltpu.CompilerParams(
            dimension_semantics=("parallel", "parallel"), vmem_limit_bytes=vmem),
        name="out_mlp",
    )(oa, ob, x, ga, gb, wo, gm, wu, wd)


def _rope_tables(S):
    inv = ROPE_THETA ** (-jnp.arange(ROPE_HALF, dtype=F32) * 2.0 / ROPE_DIM)
    ang = jnp.arange(S, dtype=F32)[:, None] * inv[None, :]
    cos, sin = jnp.cos(ang), jnp.sin(ang)
    ones = jnp.ones((S, HEAD_DIM - ROPE_DIM), F32)
    zeros = jnp.zeros((S, HEAD_DIM - ROPE_DIM), F32)
    head_cos = jnp.concatenate([cos, cos, ones], axis=1)
    head_sin = jnp.concatenate([-sin, sin, zeros], axis=1)
    rep = LANES // HEAD_DIM
    return tuple(jnp.tile(t, (1, rep)) for t in (head_cos, head_sin))


TILES = dict(tm_proj=512, tm_mlp=1024, ff_chunk=512, group_a=32, group_b=32)


def _layer(x, p):
    B, S, D = x.shape
    width_a, width_b, kv_b = p["width_a"], p["width_b"], p["kv_b"]
    cos, sin = _rope_tables(S)
    outs = _proj_call(x, p["norm_attn"], p["w_in"], p["gcat"], p["bd"], cos, sin,
                      width_a=width_a, width_b=width_b, kv_b=kv_b, tm=TILES["tm_proj"])
    qa1, ka1, va1, qa4, ka4, va4, qa16, ka16, va16, qb, kb, vb = outs
    oa = _attn_a_call((qa1, ka1, va1, qa4, ka4, va4, qa16, ka16, va16), B, S, width_a,
                      group=TILES["group_a"])
    ob = _attn_b_call(p["sink"], qb, kb, vb, group=TILES["group_b"])
    return _mlp_call(oa, ob, x, p["out_norm_a"], p["out_norm_b"], p["w_o"], p["norm_mlp"],
                     p["w_up"], p["w_down"], tm=TILES["tm_mlp"], ff_chunk=TILES["ff_chunk"])


def _prepare(norm_attn, w_in, q_norm_a, k_norm_a, q_norm_b, k_norm_b, sink_b, out_norm_a,
             out_norm_b, w_o, norm_mlp, w_up, w_down):
    width_a = out_norm_a.shape[0]
    width_b = out_norm_b.shape[0]
    kv_b = (w_in.shape[1] - 3 * width_a - width_b) // 2
    o_qa, o_ka, o_va = 0, width_a, 2 * width_a
    o_qb = 3 * width_a
    o_kb = o_qb + width_b
    o_vb = o_kb + kv_b
    w = jnp.concatenate([w_in[:, o_qa:o_qa + width_a], w_in[:, o_ka:o_ka + width_a],
                         w_in[:, o_qb:o_qb + width_b], w_in[:, o_va:o_va + width_a],
                         w_in[:, o_kb:o_kb + kv_b], w_in[:, o_vb:o_vb + kv_b]], axis=1)
    scale = HEAD_DIM ** -0.5 * LOG2E
    gcat = jnp.concatenate([jnp.tile(q_norm_a, width_a // HEAD_DIM) * scale,
                            jnp.tile(k_norm_a, width_a // HEAD_DIM),
                            jnp.tile(q_norm_b, width_b // HEAD_DIM) * scale,
                            jnp.ones((width_a,), F32),
                            jnp.tile(k_norm_b, kv_b // HEAD_DIM)])[None, :]
    n = 2 * LANES
    same_head = jnp.arange(n)[:, None] // HEAD_DIM == jnp.arange(n)[None, :] // HEAD_DIM
    bd = (same_head.astype(F32) / HEAD_DIM).astype(BF16)
    return dict(width_a=width_a, width_b=width_b, kv_b=kv_b,
                norm_attn=norm_attn[None, :], w_in=w.astype(BF16), gcat=gcat, bd=bd,
                sink=sink_b * LOG2E, out_norm_a=out_norm_a[None, :], out_norm_b=out_norm_b[None, :],
                w_o=w_o.astype(BF16), norm_mlp=norm_mlp[None, :],
                w_up=w_up.astype(BF16), w_down=w_down.astype(BF16))


def kernel(x_prompt, x_sample, norm_attn, w_in, q_norm_a, k_norm_a, q_norm_b, k_norm_b, sink_b,
           out_norm_a, out_norm_b, w_o, norm_mlp, w_up, w_down):
    yp, ys = x_prompt, x_sample
    for l in range(w_in.shape[0]):
        p = _prepare(norm_attn[l], w_in[l], q_norm_a[l], k_norm_a[l], q_norm_b[l], k_norm_b[l],
                     sink_b[l], out_norm_a[l], out_norm_b[l], w_o[l], norm_mlp[l], w_up[l],
                     w_down[l])
        yp = _layer(yp, p)
        ys = _layer(ys, p)
    return (yp, ys)
```

```python
import functools

import jax
import jax.numpy as jnp
from jax import lax
from jax.experimental import pallas as pl
from jax.experimental.pallas import tpu as pltpu

HEAD_DIM = 64
PAIR = 2 * HEAD_DIM
DILATED_BRANCHES = ((128, 1), (512, 4), (2048, 16))
WINDOW_B = 128
ROPE_THETA = 500000.0
ROPE_DIM = HEAD_DIM // 4
ROPE_HALF = ROPE_DIM // 2
EPS = 1e-6
NEG_BIG = -1e30
LOG2E = 1.4426950408889634
LANES = 128
V7X_VMEM_BYTES = 64 * 1024 * 1024

F32 = jnp.float32
BF16 = jnp.bfloat16


def _nbytes(shape, dtype):
    n = jnp.dtype(dtype).itemsize
    for s in shape:
        n *= s
    return n


def _vmem_limit(pipelined, resident, scratch, values):
    return min(2 * pipelined + resident + scratch + values, V7X_VMEM_BYTES)


def _dot(a, b):
    return jnp.dot(a, b, preferred_element_type=F32)


def _dot_nt(a, b):
    return lax.dot_general(a, b, (((1,), (1,)), ((), ())), preferred_element_type=F32)


def _proj_kernel(x_ref, g_ref, w_ref, gcat_ref, bd_ref, cos_ref, sin_ref,
                 qa1, ka1, va1, qa4, ka4, va4, qa16, ka16, va16, qb, kb, vb,
                 stage_ref, stage4_ref, *, width_a, width_b, kv_b):
    tm = x_ref.shape[0]
    x = x_ref[...]
    ms = jnp.mean(x * x, axis=-1, keepdims=True)
    xn = (x * lax.rsqrt(ms + EPS) * g_ref[...]).astype(BF16)
    cos = cos_ref[...]
    sin = sin_ref[...]
    first_half = lax.broadcasted_iota(jnp.int32, (tm, LANES), 1) % HEAD_DIM < ROPE_HALF
    bd = bd_ref[...]

    def head_norm_rope(h, col0):
        width = h.shape[1]
        sq = (h * h).astype(BF16)
        blocks = []
        sub = min(width, bd.shape[0])
        for c in range(0, width, sub):
            msq = _dot(sq[:, c:c + sub], bd[:sub, :sub])
            y = h[:, c:c + sub] * lax.rsqrt(msq + EPS) * gcat_ref[:, col0 + c:col0 + c + sub]
            for b in range(0, sub, LANES):
                yb = y[:, b:b + LANES]
                up = pltpu.roll(yb, LANES - ROPE_HALF, 1)
                dn = pltpu.roll(yb, ROPE_HALF, 1)
                blocks.append(yb * cos + jnp.where(first_half, up, dn) * sin)
        return blocks

    def write_dilated(blocks, outs, slab0):
        o1, o4, o16 = outs
        n4, n16 = tm // 4, tm // 16
        for p, blk in enumerate(blocks):
            slab = slab0 + p
            o1[p, 0] = blk.astype(BF16)
            stage_ref[slab] = blk
            for r4 in range(4):
                cls = stage_ref[slab, pl.ds(r4, n4, stride=4), :]
                o4[p, r4] = cls.astype(BF16)
                stage4_ref[slab, r4 * n4:(r4 + 1) * n4, :] = cls
            for r4 in range(4):
                for c in range(4):
                    sub = stage4_ref[slab, pl.ds(r4 * n4 + c, n16, stride=4), :]
                    o16[p, 4 * c + r4] = sub.astype(BF16)

    def write_both_halves(blk, out):
        lo = lax.broadcasted_iota(jnp.int32, (tm, LANES), 1) < HEAD_DIM
        sw = pltpu.roll(blk, HEAD_DIM, 1)
        out[0] = jnp.where(lo, blk, sw).astype(BF16)
        out[1] = jnp.where(lo, sw, blk).astype(BF16)

    def lane_blocks(h):
        return [h[:, b:b + LANES] for b in range(0, h.shape[1], LANES)]

    def write_qb(blocks):
        for p, blk in enumerate(blocks):
            qb[p] = blk.astype(BF16)

    def write_kv_b(h, c):
        write_both_halves(head_norm_rope(h[:, :kv_b], c)[0], kb)
        write_both_halves(h[:, kv_b:], vb)

    n_pa = width_a // LANES
    chunks = [
        (width_a, lambda h, c: write_dilated(head_norm_rope(h, c), (qa1, qa4, qa16), 0)),
        (width_a, lambda h, c: write_dilated(head_norm_rope(h, c), (ka1, ka4, ka16), n_pa)),
        (width_b, lambda h, c: write_qb(head_norm_rope(h, c))),
        (width_a, lambda h, c: write_dilated(lane_blocks(h), (va1, va4, va16), 2 * n_pa)),
        (2 * kv_b, write_kv_b),
    ]
    starts = [sum(w for w, _ in chunks[:i]) for i in range(len(chunks))]

    def project(i):
        return _dot(xn, w_ref[:, starts[i]:starts[i] + chunks[i][0]])

    h_next = project(0)
    for i, (_, consume) in enumerate(chunks):
        h = h_next
        if i + 1 < len(chunks):
            h_next = project(i + 1)
        consume(h, starts[i])


def _proj_call(x, g, w, gcat, bd, cos, sin, *, width_a, width_b, kv_b, tm):
    B, S, D = x.shape
    n_pa = width_a // PAIR
    n_pb = width_b // PAIR
    n_kv = kv_b // HEAD_DIM
    cols = w.shape[1]
    bf16_rows = 16
    assert [d for _, d in DILATED_BRANCHES] == [1, 4, 16]
    assert S % tm == 0 and tm % (16 * bf16_rows) == 0, (S, tm)

    def dil_shape(d):
        return jax.ShapeDtypeStruct((B, n_pa, d, S // d, PAIR), BF16)

    def dil_spec(d):
        return pl.BlockSpec((None, n_pa, d, tm // d, PAIR), lambda b, i: (b, 0, 0, i, 0))

    const = lambda b, i: (0, 0)
    out_shape = []
    out_specs = []
    for d in (1, 4, 16):
        out_shape += [dil_shape(d)] * 3
        out_specs += [dil_spec(d)] * 3
    out_shape += [jax.ShapeDtypeStruct((B, n_pb, S, PAIR), BF16),
                  jax.ShapeDtypeStruct((B, n_kv, S, PAIR), BF16),
                  jax.ShapeDtypeStruct((B, n_kv, S, PAIR), BF16)]
    out_specs += [pl.BlockSpec((None, n_pb, tm, PAIR), lambda b, i: (b, 0, i, 0)),
                  pl.BlockSpec((None, n_kv, tm, PAIR), lambda b, i: (b, 0, i, 0)),
                  pl.BlockSpec((None, n_kv, tm, PAIR), lambda b, i: (b, 0, i, 0))]
    tab_spec = pl.BlockSpec((tm, LANES), lambda b, i: (i, 0))
    out_cols = 9 * width_a + width_b + 2 * n_kv * PAIR
    vmem = _vmem_limit(
        pipelined=_nbytes((tm, D), F32) + 2 * _nbytes((tm, LANES), F32)
        + _nbytes((tm, out_cols), BF16),
        resident=_nbytes(w.shape, BF16),
        scratch=2 * _nbytes((3 * n_pa, tm, LANES), F32),
        values=_nbytes((tm, D), BF16) + 6 * _nbytes((tm, width_a), F32))
    return pl.pallas_call(
        functools.partial(_proj_kernel, width_a=width_a, width_b=width_b, kv_b=kv_b),
        grid=(B, S // tm),
        in_specs=[pl.BlockSpec((None, tm, D), lambda b, i: (b, i, 0)),
                  pl.BlockSpec((1, D), const),
                  pl.BlockSpec((D, cols), const, pipeline_mode=pl.Buffered(1)),
                  pl.BlockSpec((1, gcat.shape[1]), const),
                  pl.BlockSpec(bd.shape, const),
                  tab_spec, tab_spec],
        out_specs=out_specs,
        out_shape=out_shape,
        scratch_shapes=[pltpu.VMEM((3 * n_pa, tm, LANES), F32)] * 2,
        compiler_params=pltpu.CompilerParams(
            dimension_semantics=("parallel", "parallel"), vmem_limit_bytes=vmem),
        name="proj_qkv",
    )(x, g, w, gcat, bd, cos, sin)


BLOCKS_IN_FLIGHT = 3


def _attend(blocks, sinks, finish):
    bq = blocks[0][0][0].shape[0]
    lo = lax.broadcasted_iota(jnp.int32, (bq, PAIR), 1) < HEAD_DIM
    zero = jnp.zeros((bq, PAIR), BF16)

    def scores(i):
        q_tiles, k, _, _ = blocks[i]
        rows = []
        for q in q_tiles:
            rows.append(jnp.where(lo, q, zero))
            rows.append(jnp.where(lo, zero, q))
        return _dot_nt(jnp.concatenate(rows, axis=0), k)

    s_next = scores(0)
    for i, (q_tiles, _, v, bias) in enumerate(blocks):
        s = s_next
        if i + 1 < len(blocks):
            s_next = scores(i + 1)
        ps, ms = [], []
        for h in range(2 * len(q_tiles)):
            sh = s[h * bq:(h + 1) * bq] + bias
            m = jnp.max(sh, axis=-1, keepdims=True)
            if sinks is not None:
                m = jnp.maximum(m, sinks[h])
            ps.append(jnp.exp2(sh - m).astype(BF16))
            ms.append(m)
        pv = _dot(jnp.concatenate(ps, axis=0),
                  jnp.concatenate([v, jnp.ones_like(v)], axis=1))
        res = []
        for t in range(len(q_tiles)):
            a, b = 2 * t, 2 * t + 1
            pa, pb = pv[a * bq:(a + 1) * bq], pv[b * bq:(b + 1) * bq]
            la, lb = pa[:, PAIR:], pb[:, PAIR:]
            if sinks is not None:
                la = la + jnp.exp2(sinks[a] - ms[a])
                lb = lb + jnp.exp2(sinks[b] - ms[b])
            res.append((jnp.where(lo, pa[:, :PAIR], pb[:, :PAIR]),
                        jnp.where(lo, ms[a], ms[b]), jnp.where(lo, la, lb)))
        finish(i, res)


def _band_mask(bq, tk, offset, half_window):
    r = lax.broadcasted_iota(jnp.int32, (bq, tk), 0)
    c = lax.broadcasted_iota(jnp.int32, (bq, tk), 1)
    return jnp.abs(c - r + offset) <= half_window


def _branch_cfg(S, window, d):
    hw = window // (2 * d)
    seg = S // d
    bq = min(2 * hw, seg)
    tk = min(bq + 2 * hw, seg)
    return hw, seg, bq, tk


def _fill_bias(bias_ref, bq, tk, hw):
    for n in range(bias_ref.shape[0]):
        bias_ref[n] = jnp.where(_band_mask(bq, tk, -n * hw, hw), 0.0, NEG_BIG)


def _attn_a_kernel(q1, k1, v1, q4, k4, v4, q16, k16, v16, o_ref, st1_all, st16_all,
                   *rest, group):
    pairs, S, _ = q1.shape
    out_all, bias_refs = (None, rest) if pairs == 1 else (rest[0], rest[1:])
    assert [d for _, d in DILATED_BRANCHES] == [1, 4, 16]
    cfg = {}
    for bi, (window, d) in enumerate(DILATED_BRANCHES):
        cfg[d] = _branch_cfg(S, window, d) + (bias_refs[bi],)
        hw, seg, bq, tk, bias_ref = cfg[d]
        _fill_bias(bias_ref, bq, tk, hw)
    for t in range(pairs):
        refs = {1: (q1.at[t], k1.at[t], v1.at[t]), 4: (q4.at[t], k4.at[t], v4.at[t]),
                16: (q16.at[t], k16.at[t], v16.at[t])}
        if pairs == 1:
            _attn_a_pair(refs, cfg, st1_all.at[t], st16_all.at[t], o_ref, group)
        else:
            _attn_a_pair(refs, cfg, st1_all.at[t], st16_all.at[t], out_all.at[t], group)
            o_ref[:, t * PAIR:(t + 1) * PAIR] = out_all[t]


def _attn_a_pair(refs, cfg, st1_ref, st16_ref, o_ref, group):
    S = o_ref.shape[0]
    S4 = S // 4
    bq = cfg[1][2]
    assert cfg[4][2] == bq and cfg[16][2] == bq

    def block(d, i0):
        hw, seg, _, tk, bias_ref = cfg[d]
        q_ref, k_ref, v_ref = refs[d]
        r = i0 // seg
        m0 = i0 - r * seg
        back = m0 - jnp.clip(m0 - hw, 0, seg - tk)
        ks = pl.multiple_of(i0 - back, hw)
        return (([q_ref[pl.ds(i0, bq), :]], k_ref[pl.ds(ks, tk), :], v_ref[pl.ds(ks, tk), :],
                 bias_ref[back // hw]), (r, m0))

    def store_step(j, carry):
        blocks, dests = [], []
        for g in range(group):
            i0 = pl.multiple_of((j * group + g) * bq, bq)
            for d in (1, 16):
                blk, (r, m0) = block(d, i0)
                blocks.append(blk)
                if d == 1:
                    dests.append((st1_ref, pl.ds(i0, bq)))
                else:
                    dests.append((st16_ref, pl.ds((r % 4) * S4 + 4 * m0 + r // 4, bq, stride=4)))

        def finish(i, res):
            ref, rows = dests[i]
            for a, val in enumerate(res[0]):
                ref[a, rows, :] = val

        _attend(blocks, None, finish)
        return carry

    lax.fori_loop(0, S // (bq * group), store_step, 0)

    def merge_step(j, carry):
        blocks, rows = [], []
        for g in range(group):
            i0 = pl.multiple_of((j * group + g) * bq, bq)
            blk, (r, m0) = block(4, i0)
            blocks.append(blk)
            rows.append((pl.ds(i0, bq), pl.ds(r + 4 * m0, bq, stride=4)))

        def finish(g, res):
            own, tokens = rows[g]
            ((acc4, m4, l4),) = res
            acc1, m1, l1 = [st1_ref[a, tokens, :] for a in range(3)]
            acc16, m16, l16 = [st16_ref[a, own, :] for a in range(3)]
            m = jnp.maximum(jnp.maximum(m1, m4), m16)
            a1, a4, a16 = jnp.exp2(m1 - m), jnp.exp2(m4 - m), jnp.exp2(m16 - m)
            acc = a1 * acc1 + a4 * acc4 + a16 * acc16
            l = a1 * l1 + a4 * l4 + a16 * l16
            o_ref[tokens, :] = acc / l

        _attend(blocks, None, finish)
        return carry

    lax.fori_loop(0, S // (bq * group), merge_step, 0)


def _attn_a_call(qkv, B, S, width_a, *, group, rows):
    n_p = width_a // PAIR
    pairs = min(n_p, max(1, rows // S))
    assert n_p % pairs == 0, (n_p, pairs)
    flat = [t.reshape(B, n_p, S, PAIR) for t in qkv]
    spec = pl.BlockSpec((None, pairs, S, PAIR), lambda b, p: (b, p, 0, 0))
    scratch = [(pairs, 3, S, PAIR)] * 2 + ([(pairs, S, PAIR)] if pairs > 1 else [])
    for window, d in DILATED_BRANCHES:
        hw, seg, bq, tk = _branch_cfg(S, window, d)
        assert S % (d * hw) == 0, (S, d, hw)
        scratch.append(((tk - bq) // hw + 1, bq, tk))
    group = min(group, S // bq)
    assert S % (bq * group) == 0, (S, bq, group)
    vmem = _vmem_limit(
        pipelined=pairs * (9 * _nbytes((S, PAIR), BF16) + _nbytes((S, PAIR), F32)),
        resident=0,
        scratch=sum(_nbytes(s, F32) for s in scratch),
        values=BLOCKS_IN_FLIGHT * (_nbytes((2 * bq, 2 * bq), F32)
                                   + _nbytes((2 * bq, 2 * bq), BF16)
                                   + _nbytes((2 * bq, 2 * PAIR), F32)))
    return pl.pallas_call(
        functools.partial(_attn_a_kernel, group=group),
        grid=(B, n_p // pairs),
        in_specs=[spec] * 9,
        out_specs=pl.BlockSpec((None, S, pairs * PAIR), lambda b, p: (b, 0, p)),
        out_shape=jax.ShapeDtypeStruct((B, S, width_a), F32),
        scratch_shapes=[pltpu.VMEM(s, F32) for s in scratch],
        compiler_params=pltpu.CompilerParams(
            dimension_semantics=("parallel", "parallel"), vmem_limit_bytes=vmem),
        name="attn_dilated",
    )(*flat)


def _attn_b_kernel(sink_ref, q_ref, k_ref, v_ref, o_ref, bias_ref, *, group):
    n_t, S, _ = q_ref.shape
    kv_head = pl.program_id(1)
    bq = WINDOW_B
    tk = min(3 * bq, S)
    sinks = [sink_ref[kv_head * 2 * n_t + h] for h in range(2 * n_t)]
    _fill_bias(bias_ref, bq, tk, WINDOW_B)

    def step(j, carry):
        blocks, starts = [], []
        for g in range(group):
            i0 = pl.multiple_of((j * group + g) * bq, bq)
            ks = pl.multiple_of(jnp.clip(i0 - WINDOW_B, 0, S - tk), WINDOW_B)
            blocks.append(([q_ref[t, pl.ds(i0, bq), :] for t in range(n_t)],
                           k_ref[pl.ds(ks, tk), :], v_ref[pl.ds(ks, tk), :],
                           bias_ref[(i0 - ks) // WINDOW_B]))
            starts.append(i0)
        def finish(g, res):
            for t, (acc, m, l) in enumerate(res):
                o_ref[pl.ds(starts[g], bq), t * PAIR:(t + 1) * PAIR] = acc / l

        _attend(blocks, sinks, finish)
        return carry

    lax.fori_loop(0, S // (bq * group), step, 0)


def _attn_b_call(sink, qb, kb, vb, *, group):
    B, n_p, S, _ = qb.shape
    n_kv = kb.shape[1]
    n_t = n_p // n_kv
    kv_spec = pl.BlockSpec((None, None, S, PAIR), lambda b, g: (b, g, 0, 0))
    bq = WINDOW_B
    tk = min(3 * bq, S)
    bias_shape = ((tk - bq) // WINDOW_B + 1, bq, tk)
    rows = 2 * n_t * bq
    group = min(group, S // bq)
    assert S % (bq * group) == 0, (S, bq, group)
    vmem = _vmem_limit(
        pipelined=(n_t + 2) * _nbytes((S, PAIR), BF16) + _nbytes((S, n_t * PAIR), F32),
        resident=0,
        scratch=_nbytes(bias_shape, F32),
        values=BLOCKS_IN_FLIGHT * (_nbytes((rows, tk), F32) + _nbytes((rows, tk), BF16)
                                   + _nbytes((rows, 2 * PAIR), F32)))
    return pl.pallas_call(
        functools.partial(_attn_b_kernel, group=group),
        grid=(B, n_kv),
        in_specs=[pl.BlockSpec(memory_space=pltpu.SMEM),
                  pl.BlockSpec((None, n_t, S, PAIR), lambda b, g: (b, g, 0, 0)),
                  kv_spec, kv_spec],
        out_specs=pl.BlockSpec((None, S, n_t * PAIR), lambda b, g: (b, 0, g)),
        out_shape=jax.ShapeDtypeStruct((B, S, n_p * PAIR), F32),
        scratch_shapes=[pltpu.VMEM(bias_shape, F32)],
        compiler_params=pltpu.CompilerParams(
            dimension_semantics=("parallel", "parallel"), vmem_limit_bytes=vmem),
        name="attn_window",
    )(sink, qb, kb, vb)


def _rms(x, g):
    ms = jnp.mean(x * x, axis=-1, keepdims=True)
    return x * lax.rsqrt(ms + EPS) * g


def _mlp_kernel(oa_ref, ob_ref, x_ref, ga_ref, gb_ref, wo_ref, gm_ref, wu_ref, wd_ref, y_ref,
                *, ff_chunk):
    wa = oa_ref.shape[1]
    na = _rms(oa_ref[...], ga_ref[...]).astype(BF16)
    nb = _rms(ob_ref[...], gb_ref[...]).astype(BF16)
    h = x_ref[...] + _dot(na, wo_ref[:wa, :]) + _dot(nb, wo_ref[wa:, :])
    hn = _rms(h, gm_ref[...]).astype(BF16)
    acc = None
    for c in range(0, wu_ref.shape[1], ff_chunk):
        u = jnp.maximum(_dot(hn, wu_ref[:, c:c + ff_chunk]), 0.0)
        d = _dot((u * u).astype(BF16), wd_ref[c:c + ff_chunk, :])
        acc = d if acc is None else acc + d
    y_ref[...] = h + acc


def _mlp_call(oa, ob, x, ga, gb, wo, gm, wu, wd, *, tm, ff_chunk):
    B, S, D = x.shape
    const = lambda b, i: (0, 0)

    def resident(a):
        return pl.BlockSpec(a.shape, const, pipeline_mode=pl.Buffered(1))

    def rows(a):
        return pl.BlockSpec((None, tm, a.shape[2]), lambda b, i: (b, i, 0))

    assert S % tm == 0 and wu.shape[1] % ff_chunk == 0, (S, tm, wu.shape, ff_chunk)
    vmem = _vmem_limit(
        pipelined=_nbytes((tm, oa.shape[2] + ob.shape[2] + 2 * D), F32),
        resident=sum(_nbytes(a.shape, a.dtype) for a in (wo, wu, wd)),
        scratch=0,
        values=2 * _nbytes((tm, D), F32) + _nbytes((tm, D), BF16)
        + _nbytes((tm, ff_chunk), F32) + _nbytes((tm, ff_chunk), BF16))
    return pl.pallas_call(
        functools.partial(_mlp_kernel, ff_chunk=ff_chunk),
        grid=(B, S // tm),
        in_specs=[rows(oa), rows(ob), rows(x), resident(ga), resident(gb), resident(wo),
                  resident(gm), resident(wu), resident(wd)],
        out_specs=rows(x),
        out_shape=jax.ShapeDtypeStruct((B, S, D), F32),
        compiler_params=pltpu.CompilerParams(
            dimension_semantics=("parallel", "parallel"), vmem_limit_bytes=vmem),
        name="out_mlp",
    )(oa, ob, x, ga, gb, wo, gm, wu, wd)


def _rope_tables(S):
    inv = ROPE_THETA ** (-jnp.arange(ROPE_HALF, dtype=F32) * 2.0 / ROPE_DIM)
    ang = jnp.arange(S, dtype=F32)[:, None] * inv[None, :]
    cos, sin = jnp.cos(ang), jnp.sin(ang)
    ones = jnp.ones((S, HEAD_DIM - ROPE_DIM), F32)
    zeros = jnp.zeros((S, HEAD_DIM - ROPE_DIM), F32)
    head_cos = jnp.concatenate([cos, cos, ones], axis=1)
    head_sin = jnp.concatenate([-sin, sin, zeros], axis=1)
    rep = LANES // HEAD_DIM
    return tuple(jnp.tile(t, (1, rep)) for t in (head_cos, head_sin))


TILES = dict(tm_proj=512, tm_mlp=1024, ff_chunk=512, group_a=32, group_b=32, rows_a=4096)


def _layer(x, p):
    B, S, D = x.shape
    width_a, width_b, kv_b = p["width_a"], p["width_b"], p["kv_b"]
    cos, sin = _rope_tables(S)
    outs = _proj_call(x, p["norm_attn"], p["w_in"], p["gcat"], p["bd"], cos, sin,
                      width_a=width_a, width_b=width_b, kv_b=kv_b, tm=TILES["tm_proj"])
    qa1, ka1, va1, qa4, ka4, va4, qa16, ka16, va16, qb, kb, vb = outs
    oa = _attn_a_call((qa1, ka1, va1, qa4, ka4, va4, qa16, ka16, va16), B, S, width_a,
                      group=TILES["group_a"], rows=TILES["rows_a"])
    ob = _attn_b_call(p["sink"], qb, kb, vb, group=TILES["group_b"])
    return _mlp_call(oa, ob, x, p["out_norm_a"], p["out_norm_b"], p["w_o"], p["norm_mlp"],
                     p["w_up"], p["w_down"], tm=TILES["tm_mlp"], ff_chunk=TILES["ff_chunk"])


def _prepare(norm_attn, w_in, q_norm_a, k_norm_a, q_norm_b, k_norm_b, sink_b, out_norm_a,
             out_norm_b, w_o, norm_mlp, w_up, w_down):
    width_a = out_norm_a.shape[0]
    width_b = out_norm_b.shape[0]
    kv_b = (w_in.shape[1] - 3 * width_a - width_b) // 2
    o_qa, o_ka, o_va = 0, width_a, 2 * width_a
    o_qb = 3 * width_a
    o_kb = o_qb + width_b
    o_vb = o_kb + kv_b
    w = jnp.concatenate([w_in[:, o_qa:o_qa + width_a], w_in[:, o_ka:o_ka + width_a],
                         w_in[:, o_qb:o_qb + width_b], w_in[:, o_va:o_va + width_a],
                         w_in[:, o_kb:o_kb + kv_b], w_in[:, o_vb:o_vb + kv_b]], axis=1)
    scale = HEAD_DIM ** -0.5 * LOG2E
    gcat = jnp.concatenate([jnp.tile(q_norm_a, width_a // HEAD_DIM) * scale,
                            jnp.tile(k_norm_a, width_a // HEAD_DIM),
                            jnp.tile(q_norm_b, width_b // HEAD_DIM) * scale,
                            jnp.ones((width_a,), F32),
                            jnp.tile(k_norm_b, kv_b // HEAD_DIM)])[None, :]
    n = 2 * LANES
    same_head = jnp.arange(n)[:, None] // HEAD_DIM == jnp.arange(n)[None, :] // HEAD_DIM
    bd = (same_head.astype(F32) / HEAD_DIM).astype(BF16)
    return dict(width_a=width_a, width_b=width_b, kv_b=kv_b,
                norm_attn=norm_attn[None, :], w_in=w.astype(BF16), gcat=gcat, bd=bd,
                sink=sink_b * LOG2E, out_norm_a=out_norm_a[None, :], out_norm_b=out_norm_b[None, :],
                w_o=w_o.astype(BF16), norm_mlp=norm_mlp[None, :],
                w_up=w_up.astype(BF16), w_down=w_down.astype(BF16))


def kernel(x_prompt, x_sample, norm_attn, w_in, q_norm_a, k_norm_a, q_norm_b, k_norm_b, sink_b,
           out_norm_a, out_norm_b, w_o, norm_mlp, w_up, w_down):
    yp, ys = x_prompt, x_sample
    for l in range(w_in.shape[0]):
        p = _prepare(norm_attn[l], w_in[l], q_norm_a[l], k_norm_a[l], q_norm_b[l], k_norm_b[l],
                     sink_b[l], out_norm_a[l], out_norm_b[l], w_o[l], norm_mlp[l], w_up[l],
                     w_down[l])
        yp = _layer(yp, p)
        ys = _layer(ys, p)
    return (yp, ys)
```
